```python
import jax, jax.numpy as jnp
from jax import lax
import numpy as np

D_MODEL = 1024
BATCH = 8
SEQ = 2048
DEPTH = 2

HEAD_DIM = 64
BLOCK = 128
ROPE_THETA = 10000.0
EPS = 1e-6
FOX_HEADS = 8
DSA_HEADS = 8
IDX_HEADS = 4
IDX_DIM = 64
DSA_TOPK_MAX = 256
RET_HEADS = 8
SWA_Q_HEADS = 8
SWA_KV_HEADS = 2
SWA_WINDOW = 128
N_BRANCH = 4
BRANCH_WIDTH = 8 * HEAD_DIM
D_FF = 2816
CONV_WIDTH = 3

COL_SIZES = (
    FOX_HEADS * HEAD_DIM, FOX_HEADS * HEAD_DIM, FOX_HEADS * HEAD_DIM, FOX_HEADS,
    DSA_HEADS * HEAD_DIM, HEAD_DIM, HEAD_DIM, IDX_HEADS * IDX_DIM, IDX_DIM, IDX_HEADS,
    RET_HEADS * HEAD_DIM, RET_HEADS * HEAD_DIM, RET_HEADS * HEAD_DIM, RET_HEADS * HEAD_DIM,
    SWA_Q_HEADS * HEAD_DIM, SWA_KV_HEADS * HEAD_DIM, SWA_KV_HEADS * HEAD_DIM,
    N_BRANCH * D_MODEL,
)
N_IN = sum(COL_SIZES)

kernel_name = "hybrid_fox_dsa_retention_swa_gated"


def rmsnorm(x, g):
    x32 = x.astype(jnp.float32)
    y = x32 * lax.rsqrt(jnp.mean(x32 * x32, axis=-1, keepdims=True) + EPS)
    return (y * g.astype(jnp.float32)).astype(x.dtype)


def rope(x, pos):
    d = x.shape[-1]
    half = d // 2
    inv = 1.0 / (ROPE_THETA ** (jnp.arange(half, dtype=jnp.float32) * 2.0 / d))
    ang = pos[:, None] * inv[None, :]
    cos = jnp.cos(ang)[None, :, None, :]
    sin = jnp.sin(ang)[None, :, None, :]
    x32 = x.astype(jnp.float32)
    x1, x2 = x32[..., :half], x32[..., half:]
    return jnp.concatenate([x1 * cos - x2 * sin, x2 * cos + x1 * sin], axis=-1).astype(x.dtype)


def to_blocks(t, nb):
    return t.reshape(t.shape[0], nb, BLOCK, *t.shape[2:]).swapaxes(0, 1)


def from_blocks(t):
    t = t.swapaxes(0, 1)
    return t.reshape(t.shape[0], t.shape[1] * t.shape[2], -1)


def forgetting_attention(q, k, v, f_logit, f_bias):
    B, S, H, d = q.shape
    nb = S // BLOCK
    neg = jnp.finfo(jnp.float32).min
    logf = jax.nn.log_sigmoid((f_logit + f_bias).astype(jnp.float32))
    F = jnp.cumsum(logf, axis=1).transpose(0, 2, 1)
    Fq_blocks = F.reshape(B, H, nb, BLOCK).transpose(2, 0, 1, 3)
    key_pos = jnp.arange(S, dtype=jnp.int32)
    scale = d ** -0.5

    def blk(inp):
        qb, Fq, off = inp
        t = off + jnp.arange(BLOCK, dtype=jnp.int32)
        logits = (jnp.einsum('bqhd,bshd->bhqs', qb, k).astype(jnp.float32) * scale
                  + Fq[..., None] - F[:, :, None, :])
        causal = key_pos[None, :] <= t[:, None]
        p = jax.nn.softmax(jnp.where(causal[None, None], logits, neg), axis=-1)
        return jnp.einsum('bhqs,bshd->bqhd', p.astype(v.dtype), v)

    offs = jnp.arange(nb, dtype=jnp.int32) * BLOCK
    out = lax.map(blk, (to_blocks(q, nb), Fq_blocks, offs))
    return from_blocks(out)


def dsa_attention(q, k, v, qi, ki, wi):
    B, S, H, d = q.shape
    nb = S // BLOCK
    topk = min(DSA_TOPK_MAX, S // 4)
    neg = jnp.finfo(jnp.float32).min
    key_pos = jnp.arange(S, dtype=jnp.int32)
    bidx = jnp.arange(B, dtype=jnp.int32)[:, None, None]
    ki32 = ki.astype(jnp.float32)
    w_scale = (IDX_HEADS ** -0.5) * (IDX_DIM ** -0.5)
    scale = d ** -0.5

    def blk(inp):
        qb, qib, wb, off = inp
        t = off + jnp.arange(BLOCK, dtype=jnp.int32)
        dots = jnp.einsum('bqhd,bsd->bqhs', qib.astype(jnp.float32), ki32)
        score = jnp.einsum('bqhs,bqh->bqs', jax.nn.relu(dots), wb.astype(jnp.float32) * w_scale)
        causal = key_pos[None, :] <= t[:, None]
        score = jnp.where(causal[None], score, neg)
        _, sel = lax.top_k(score, topk)
        kg = k[bidx, sel]
        vg = v[bidx, sel]
        logits = jnp.einsum('bqhd,bqkd->bhqk', qb, kg).astype(jnp.float32) * scale
        valid = (sel <= t[None, :, None])[:, None]
        p = jax.nn.softmax(jnp.where(valid, logits, neg), axis=-1)
        return jnp.einsum('bhqk,bqkd->bqhd', p.astype(vg.dtype), vg)

    offs = jnp.arange(nb, dtype=jnp.int32) * BLOCK
    out = lax.map(blk, (to_blocks(q, nb), to_blocks(qi, nb), to_blocks(wi, nb), offs))
    return from_blocks(out)


def retention(q, k, v):
    B, S, H, d = q.shape
    dv = v.shape[-1]
    C = BLOCK
    nc = S // C
    log_g = jnp.log(1.0 - 2.0 ** (-5.0 - jnp.arange(H, dtype=jnp.float32)))
    n = jnp.arange(C, dtype=jnp.float32)
    diff = n[:, None] - n[None, :]
    Dm = jnp.where(diff[None] >= 0, jnp.exp(diff[None] * log_g[:, None, None]), 0.0)
    xi = jnp.exp((n + 1.0)[None, :] * log_g[:, None])
    zeta = jnp.exp((C - 1.0 - n)[None, :] * log_g[:, None])
    gC = jnp.exp(C * log_g)

    def chunks(t):
        return t.reshape(B, nc, C, H, t.shape[-1]).transpose(1, 0, 3, 2, 4)

    def step(R, inp):
        qc, kc, vc = inp
        inner = jnp.einsum('bhnm,bhme->bhne', jnp.einsum('bhnd,bhmd->bhnm', qc, kc) * Dm[None], vc)
        cross = jnp.einsum('bhnd,bhde->bhne', qc, R) * xi[None, :, :, None]
        R = R * gC[None, :, None, None] + jnp.einsum('bhmd,bhme->bhde', kc * zeta[None, :, :, None], vc)
        return R, inner + cross

    R0 = jnp.zeros((B, H, d, dv), jnp.float32)
    _, ys = lax.scan(step, R0, (chunks(q), chunks(k), chunks(v)))
    return ys.transpose(1, 0, 3, 2, 4).reshape(B, S, H, dv)


def head_groupnorm(r, g):
    mu = jnp.mean(r, axis=-1, keepdims=True)
    var = jnp.mean(jnp.square(r - mu), axis=-1, keepdims=True)
    y = (r - mu) * lax.rsqrt(var + EPS)
    return y.reshape(r.shape[0], r.shape[1], -1) * g.astype(jnp.float32)


def sliding_window_gqa(q, k, v, sinks):
    B, S, HQ, d = q.shape
    HKV = k.shape[2]
    G = HQ // HKV
    W = SWA_WINDOW
    nb = S // W
    neg = jnp.finfo(jnp.float32).min
    qb = q.reshape(B, nb, W, HKV, G, d)
    kb = k.reshape(B, nb, W, HKV, d)
    vb = v.reshape(B, nb, W, HKV, d)
    shift = lambda t: jnp.concatenate([jnp.zeros_like(t[:, :1]), t[:, :-1]], axis=1)
    kband = jnp.concatenate([shift(kb), kb], axis=2)
    vband = jnp.concatenate([shift(vb), vb], axis=2)
    logits = jnp.einsum('bnqgrd,bnkgd->bngrqk', qb, kband).astype(jnp.float32) * (d ** -0.5)
    i = jnp.arange(W)[:, None]
    j = jnp.arange(2 * W)[None, :]
    in_band = (j > i) & (j <= i + W)
    kpos = jnp.arange(nb)[:, None, None] * W - W + j[None]
    mask = in_band[None] & (kpos >= 0)
    logits = jnp.where(mask[None, :, None, None], logits, neg)
    sink = sinks.astype(jnp.float32).reshape(HKV, G)[None, None, :, :, None, None]
    m = jnp.maximum(jnp.max(logits, axis=-1, keepdims=True), sink)
    p = jnp.exp(logits - m)
    denom = jnp.sum(p, axis=-1, keepdims=True) + jnp.exp(sink - m)
    out = jnp.einsum('bngrqk,bnkgd->bnqgrd', (p / denom).astype(vband.dtype), vband)
    return out.reshape(B, S, HQ * d)


def causal_dwconv(u, w, b):
    C = u.shape[-1]
    out = lax.conv_general_dilated(u, w.astype(u.dtype)[:, None, :], window_strides=(1,),
                                   padding=[(CONV_WIDTH - 1, 0)],
                                   dimension_numbers=('NWC', 'WIO', 'NWC'),
                                   feature_group_count=C)
    return out + b.astype(u.dtype)


def setup_inputs(seed: int = 0) -> dict:
    key = jax.random.key(seed)
    ks = jax.random.split(key, 14)
    nrm = lambda k, shape: jax.random.normal(k, shape, jnp.float32)
    return {
        "x": nrm(ks[0], (BATCH, SEQ, D_MODEL)),
        "attn_norm": 1.0 + 0.02 * nrm(ks[1], (DEPTH, D_MODEL)),
        "w_in": nrm(ks[2], (DEPTH, D_MODEL, N_IN)) * D_MODEL ** -0.5,
        "forget_bias": 3.0 + 0.1 * nrm(ks[3], (DEPTH, FOX_HEADS)),
        "ret_norm": 1.0 + 0.02 * nrm(ks[4], (DEPTH, RET_HEADS * HEAD_DIM)),
        "attn_sinks": 0.1 * nrm(ks[5], (DEPTH, SWA_Q_HEADS)),
        "w_branch": nrm(ks[6], (DEPTH, N_BRANCH, BRANCH_WIDTH, D_MODEL)) * BRANCH_WIDTH ** -0.5,
        "w_out": nrm(ks[7], (DEPTH, D_MODEL, D_MODEL)) * D_MODEL ** -0.5,
        "ffn_norm": 1.0 + 0.02 * nrm(ks[8], (DEPTH, D_MODEL)),
        "w_up": nrm(ks[9], (DEPTH, D_MODEL, 2 * D_FF)) * D_MODEL ** -0.5,
        "conv_w": nrm(ks[10], (DEPTH, CONV_WIDTH, 2 * D_FF)) * CONV_WIDTH ** -0.5,
        "conv_b": 0.01 * nrm(ks[11], (DEPTH, 2 * D_FF)),
        "w_down": nrm(ks[12], (DEPTH, D_FF, D_MODEL)) * D_FF ** -0.5,
        "final_norm": 1.0 + 0.02 * nrm(ks[13], (D_MODEL,)),
    }


def reference(x, attn_norm, w_in, forget_bias, ret_norm, attn_sinks, w_branch, w_out,
              ffn_norm, w_up, conv_w, conv_b, w_down, final_norm):
    B, S, D = x.shape
    pos = jnp.arange(S, dtype=jnp.float32)
    split_at = np.cumsum(COL_SIZES)[:-1].tolist()
    heads = lambda t, n: t.reshape(B, S, n, -1)
    for l in range(DEPTH):
        h = rmsnorm(x, attn_norm[l])
        z = h @ w_in[l]
        (qA, kA, vA, fA, qB, kB, vB, qI, kI, wI,
         qC, kC, vC, gC, qD, kD, vD, gates) = jnp.split(z, split_at, axis=-1)

        yA = forgetting_attention(heads(qA, FOX_HEADS), heads(kA, FOX_HEADS), heads(vA, FOX_HEADS),
                                  fA, forget_bias[l])
        yB = dsa_attention(rope(heads(qB, DSA_HEADS), pos), rope(kB[:, :, None], pos)[:, :, 0], vB,
                           rope(heads(qI, IDX_HEADS), pos), rope(kI[:, :, None], pos)[:, :, 0], wI)
        r = retention(rope(heads(qC, RET_HEADS), pos).astype(jnp.float32),
                      (rope(heads(kC, RET_HEADS), pos) * HEAD_DIM ** -0.5).astype(jnp.float32),
                      heads(vC, RET_HEADS).astype(jnp.float32))
        yC = (head_groupnorm(r, ret_norm[l]) * jax.nn.silu(gC.astype(jnp.float32))).astype(x.dtype)
        yD = sliding_window_gqa(rope(heads(qD, SWA_Q_HEADS), pos), rope(heads(kD, SWA_KV_HEADS), pos),
                                heads(vD, SWA_KV_HEADS), attn_sinks[l])

        ys = jnp.stack([yA, yB, yC, yD], axis=2)
        branch = jnp.einsum('bsnc,ncd->bsnd', ys, w_branch[l])
        g = jax.nn.sigmoid(gates.reshape(B, S, N_BRANCH, D))
        merged = jnp.sum(g * branch, axis=2)
        x = x + merged @ w_out[l]

        h = rmsnorm(x, ffn_norm[l])
        u = causal_dwconv(h @ w_up[l], conv_w[l], conv_b[l])
        a, b = jnp.split(u, 2, axis=-1)
        x = x + (jax.nn.silu(a) * b) @ w_down[l]
    return rmsnorm(x, final_norm)
```

```python
import functools

import numpy as np
import jax
import jax.numpy as jnp
from jax import lax
from jax.experimental import pallas as pl
from jax.experimental.pallas import tpu as pltpu

F32 = jnp.float32
BF16 = jnp.bfloat16
I32 = jnp.int32

LANES = 128
HEAD_DIM = 64
HALF = HEAD_DIM // 2
HEADS = 8
BRANCH_WIDTH = HEADS * HEAD_DIM
IDX_HEADS = 4
SWA_KV_HEADS = 2
SWA_WINDOW = 128
RET_CHUNK = 128
N_BRANCH = 4
CONV_WIDTH = 3
DSA_TOPK_MAX = 256
ROPE_THETA = 10000.0
EPS = 1e-6
VMEM_LIMIT = 56 * 1024 * 1024

M_INIT = -1e30
MASKED = -2e30
INT_MIN = -2 ** 31

CB_QA, CB_KA, CB_VA, CB_VC, CB_GC = 0, 4, 8, 12, 16
CB_VB, CB_VD = 20, 22
N_PLAIN_BLOCKS = 24
CB_QB, CB_QC, CB_KC, CB_QD, CB_QI = 24, 28, 32, 36, 40
CB_KB, CB_KI, CB_KD = 42, 43, 44
N_BLOCKS = 46
TN_IN = 256

_SIZES = (512, 512, 512, 8, 512, 64, 64, 256, 64, 4, 512, 512, 512, 512, 512, 128, 128, 4096)
_OFFS = np.concatenate([[0], np.cumsum(_SIZES)]).tolist()
(O_QA, O_KA, O_VA, O_FA, O_QB, O_KB, O_VB, O_QI, O_KI, O_WI,
 O_QC, O_KC, O_VC, O_GC, O_QD, O_KD, O_VD, O_GATES, N_IN) = _OFFS
ZS_WI = 8


def _column_plan():
    perm = np.full((N_BLOCKS * LANES,), N_IN, np.int32)
    scale = np.ones((N_BLOCKS * LANES,), np.float32)

    def put(block, off, width, s=1.0):
        perm[block * LANES: block * LANES + width] = np.arange(off, off + width)
        scale[block * LANES: block * LANES + width] = s

    q_scale = HEAD_DIM ** -0.5
    put(CB_QA, O_QA, 512, q_scale); put(CB_KA, O_KA, 512); put(CB_VA, O_VA, 512)
    put(CB_VC, O_VC, 512); put(CB_GC, O_GC, 512)
    for half in range(2):
        perm[CB_VB * LANES + half * 64: CB_VB * LANES + half * 64 + 64] = np.arange(O_VB, O_VB + 64)
        perm[CB_KB * LANES + half * 64: CB_KB * LANES + half * 64 + 64] = np.arange(O_KB, O_KB + 64)
        perm[CB_KI * LANES + half * 64: CB_KI * LANES + half * 64 + 64] = np.arange(O_KI, O_KI + 64)
        for g in range(SWA_KV_HEADS):
            perm[(CB_VD + g) * LANES + half * 64: (CB_VD + g) * LANES + half * 64 + 64] = np.arange(O_VD + 64 * g, O_VD + 64 * g + 64)
            perm[(CB_KD + g) * LANES + half * 64: (CB_KD + g) * LANES + half * 64 + 64] = np.arange(O_KD + 64 * g, O_KD + 64 * g + 64)
    put(CB_QB, O_QB, 512, q_scale); put(CB_QC, O_QC, 512); put(CB_KC, O_KC, 512, q_scale)
    put(CB_QD, O_QD, 512, q_scale); put(CB_QI, O_QI, 256)
    return perm, scale


def _rmsnorm_f32(x, g):
    ms = jnp.mean(x * x, axis=-1, keepdims=True)
    return x * lax.rsqrt(ms + EPS) * g


def _lane_lt_half_tile(shape):
    return lax.broadcasted_iota(I32, shape, len(shape) - 1) < HEAD_DIM


def _dot_nt(a, b):
    return lax.dot_general(a, b, (((1,), (1,)), ((), ())), preferred_element_type=F32)


def _dot(a, b):
    return jnp.dot(a, b, preferred_element_type=F32)


def _inproj_kernel(x_ref, g_ref, w_ref, ws_ref, cos_ref, sin_ref, scale_ref, z_ref, zs_ref, h_ref, *, n_plain_tiles):
    j = pl.program_id(1)

    @pl.when(j == 0)
    def _():
        hb = _rmsnorm_f32(x_ref[...], g_ref[...]).astype(BF16)
        h_ref[...] = hb
        zs_ref[...] = _dot(hb, ws_ref[...])

    acc = _dot(h_ref[...], w_ref[...]) * scale_ref[...]

    @pl.when(j < n_plain_tiles)
    def _():
        z_ref[...] = acc.astype(BF16)

    @pl.when(j >= n_plain_tiles)
    def _():
        tm = acc.shape[0]
        lane = lax.broadcasted_iota(I32, (tm, LANES), 1)
        first_half = (lane % HEAD_DIM) < HALF
        for c in range(acc.shape[1] // LANES):
            a = acc[:, c * LANES:(c + 1) * LANES]
            rot = jnp.where(first_half, pltpu.roll(a, LANES - HALF, 1), pltpu.roll(a, HALF, 1))
            z_ref[:, c * LANES:(c + 1) * LANES] = (a * cos_ref[...] + rot * sin_ref[...]).astype(BF16)


def _inproj(x2, g, w_main, w_small, cos_t, sin_t, scale, seq):
    T, D = x2.shape
    N = w_main.shape[1]
    tm = min(1024, seq)
    tn = TN_IN
    pos_blocks = seq // tm
    return pl.pallas_call(
        functools.partial(_inproj_kernel, n_plain_tiles=N_PLAIN_BLOCKS * LANES // tn),
        grid=(T // tm, N // tn),
        in_specs=[
            pl.BlockSpec((tm, D), lambda i, j: (i, 0)),
            pl.BlockSpec((1, D), lambda i, j: (0, 0)),
            pl.BlockSpec((D, tn), lambda i, j: (0, j)),
            pl.BlockSpec((D, LANES), lambda i, j: (0, 0)),
            pl.BlockSpec((tm, LANES), lambda i, j: (i % pos_blocks, 0)),
            pl.BlockSpec((tm, LANES), lambda i, j: (i % pos_blocks, 0)),
            pl.BlockSpec((1, tn), lambda i, j: (0, j)),
        ],
        out_specs=[
            pl.BlockSpec((tm, tn), lambda i, j: (i, j)),
            pl.BlockSpec((tm, LANES), lambda i, j: (i, 0)),
        ],
        out_shape=[jax.ShapeDtypeStruct((T, N), BF16), jax.ShapeDtypeStruct((T, LANES), F32)],
        scratch_shapes=[pltpu.VMEM((tm, D), BF16)],
        compiler_params=pltpu.CompilerParams(dimension_semantics=("parallel", "arbitrary"), vmem_limit_bytes=VMEM_LIMIT),
        name="inproj",
    )(x2, g, w_main, w_small, cos_t, sin_t, scale)


def _fcum_kernel(zs_ref, bias_ref, tri_ref, f_ref):
    n_chunks = zs_ref.shape[0] // LANES

    def body(c, carry):
        rows = pl.ds(pl.multiple_of(c * LANES, LANES), LANES)
        xv = zs_ref[rows, :] + bias_ref[...]
        logf = jnp.minimum(xv, 0.0) - jnp.log(1.0 + jnp.exp(-jnp.abs(xv)))
        hi = logf.astype(BF16)
        r1 = logf - hi.astype(F32)
        mid = r1.astype(BF16)
        lo = (r1 - mid.astype(F32)).astype(BF16)
        tri = tri_ref[...]
        cs = _dot(tri, hi) + _dot(tri, mid) + _dot(tri, lo) + carry
        f_ref[rows, :] = cs
        return cs[LANES - 1:LANES, :]

    lax.fori_loop(0, n_chunks, body, jnp.zeros((1, LANES), F32))


def _fcum(zs, bias_row, seq):
    T = zs.shape[0]
    tri = jnp.asarray(np.tril(np.ones((LANES, LANES), np.float32)), BF16)
    return pl.pallas_call(
        _fcum_kernel,
        grid=(T // seq,),
        in_specs=[
            pl.BlockSpec((seq, LANES), lambda b: (b, 0)),
            pl.BlockSpec((1, LANES), lambda b: (0, 0)),
            pl.BlockSpec((LANES, LANES), lambda b: (0, 0)),
        ],
        out_specs=pl.BlockSpec((seq, LANES), lambda b: (b, 0)),
        out_shape=jax.ShapeDtypeStruct((T, LANES), F32),
        compiler_params=pltpu.CompilerParams(dimension_semantics=("parallel",)),
        name="fcum",
    )(zs, bias_row, tri)


def _fox_kernel(q_ref, k_ref, v_ref, fcol_ref, frow_ref, y_ref, *, tq, tk):
    i = pl.program_id(1)
    n_kc = (i * tq + tq + tk - 1) // tk
    row = i * tq + lax.broadcasted_iota(I32, (tq, 1), 0)
    lo_half = _lane_lt_half_tile((tq, LANES))
    for p in range(HEADS // 2):
        cols = slice(p * LANES, (p + 1) * LANES)
        q2 = q_ref[:, cols]
        zero = jnp.zeros_like(q2)
        q_heads = (jnp.where(lo_half, q2, zero), jnp.where(lo_half, zero, q2))
        fq = (fcol_ref[:, 2 * p:2 * p + 1], fcol_ref[:, 2 * p + 1:2 * p + 2])

        def body(kc, carry, cols=cols, q_heads=q_heads, fq=fq, p=p):
            k0 = pl.multiple_of(kc * tk, tk)
            k2 = k_ref[pl.ds(k0, tk), cols]
            v2 = v_ref[pl.ds(k0, tk), cols]
            causal = (k0 + lax.broadcasted_iota(I32, (1, tk), 1)) <= row
            out = []
            for e in range(2):
                m, l, acc = carry[e]
                fk = frow_ref[0, 2 * p + e, pl.ds(kc, 1), :]
                s = _dot_nt(q_heads[e], k2) + fq[e] - fk
                s = jnp.where(causal, s, MASKED)
                m_new = jnp.maximum(m, jnp.max(s, axis=-1, keepdims=True))
                alpha = jnp.exp(m - m_new)
                pexp = jnp.exp(s - m_new)
                l = alpha * l + jnp.sum(pexp, axis=-1, keepdims=True)
                acc = alpha * acc + _dot(pexp.astype(BF16), v2)
                out.append((m_new, l, acc))
            return tuple(out)

        init = tuple((jnp.full((tq, 1), M_INIT, F32), jnp.zeros((tq, 1), F32), jnp.zeros((tq, LANES), F32))
                     for _ in range(2))
        (_, l0, a0), (_, l1, a1) = lax.fori_loop(0, n_kc, body, init)
        y_ref[:, cols] = jnp.where(lo_half, a0 / l0, a1 / l1).astype(BF16)


def _fox(z, fcol, frow, batch, seq, tq, tk):
    nq = seq // tq
    T = batch * seq
    return pl.pallas_call(
        functools.partial(_fox_kernel, tq=tq, tk=tk),
        grid=(batch, nq),
        in_specs=[
            pl.BlockSpec((tq, BRANCH_WIDTH), lambda b, i: (b * nq + i, CB_QA // 4)),
            pl.BlockSpec((seq, BRANCH_WIDTH), lambda b, i: (b, CB_KA // 4)),
            pl.BlockSpec((seq, BRANCH_WIDTH), lambda b, i: (b, CB_VA // 4)),
            pl.BlockSpec((tq, LANES), lambda b, i: (b * nq + i, 0)),
            pl.BlockSpec((1, HEADS, seq // tk, tk), lambda b, i: (b, 0, 0, 0)),
        ],
        out_specs=pl.BlockSpec((tq, BRANCH_WIDTH), lambda b, i: (b * nq + i, 0)),
        out_shape=jax.ShapeDtypeStruct((T, BRANCH_WIDTH), BF16),
        compiler_params=pltpu.CompilerParams(dimension_semantics=("parallel", "arbitrary"), vmem_limit_bytes=VMEM_LIMIT),
        name="fox",
    )(z, z, z, fcol, frow)


def _dsa_kernel(qi_ref, kk_ref, vv_ref, qb_ref, zs_ref, triu_ref, y_ref, key_ref, bias_ref, qi4_ref, q8_ref,
                *, tq, ck, topk):
    i = pl.program_id(1)
    off = i * tq
    n_kc = (off + tq + ck - 1) // ck
    row = off + lax.broadcasted_iota(I32, (tq, 1), 0)
    lo_half = _lane_lt_half_tile((tq, LANES))

    for h in range(IDX_HEADS):
        t = qi_ref[:, (h // 2) * LANES:(h // 2 + 1) * LANES]
        keep = lo_half if h % 2 == 0 else jnp.logical_not(lo_half)
        qi4_ref[h * tq:(h + 1) * tq, :] = jnp.where(keep, t, jnp.zeros_like(t))
    for h in range(HEADS):
        t = qb_ref[:, (h // 2) * LANES:(h // 2 + 1) * LANES]
        keep = lo_half if h % 2 == 0 else jnp.logical_not(lo_half)
        q8_ref[h * tq:(h + 1) * tq, :] = jnp.where(keep, t, jnp.zeros_like(t))
    w_scale = (IDX_HEADS ** -0.5) * (HEAD_DIM ** -0.5)
    w_heads = [zs_ref[:, ZS_WI + h:ZS_WI + h + 1] * w_scale for h in range(IDX_HEADS)]

    def score_body(kc, _):
        k0 = pl.multiple_of(kc * ck, ck)
        ki = kk_ref[pl.ds(k0, ck), LANES:2 * LANES]
        d = _dot_nt(qi4_ref[...], ki)
        s = jnp.zeros((tq, ck), F32)
        for h in range(IDX_HEADS):
            s = s + jnp.maximum(d[h * tq:(h + 1) * tq, :], 0.0) * w_heads[h]
        s = jnp.where(s == 0.0, 0.0, s)
        bits = pltpu.bitcast(s, I32)
        key = bits ^ ((bits >> 31) & 0x7FFFFFFF)
        causal = (k0 + lax.broadcasted_iota(I32, (1, ck), 1)) <= row
        key_ref[kc] = jnp.where(causal, key, INT_MIN)
        return 0

    lax.fori_loop(0, n_kc, score_body, 0)

    def count(pred):
        def body(kc, acc):
            hit = jnp.where(pred(key_ref[kc]), 1.0, 0.0)
            for c in range(ck // LANES):
                acc = acc + hit[:, c * LANES:(c + 1) * LANES]
            return acc
        acc = lax.fori_loop(0, n_kc, body, jnp.zeros((tq, LANES), F32))
        return jnp.sum(acc, axis=-1, keepdims=True)

    kf = float(topk)
    c0 = count(lambda k: k >= 0)
    v0 = jnp.where(c0 >= kf, 0, INT_MIN).astype(I32)

    def bit_body(b, v):
        cand = v | jnp.left_shift(jnp.int32(1), 30 - b)
        c = count(lambda k: k >= cand)
        return jnp.where(c >= kf, cand, v)

    v = lax.fori_loop(0, 31, bit_body, v0)
    need = kf - count(lambda k: k > v)

    def bias_body(kc, carry):
        key = key_ref[kc]
        eq = (key == v) & (key > INT_MIN)
        eqf = jnp.where(eq, 1.0, 0.0)
        prefix = _dot(eqf.astype(BF16), triu_ref[...]) + carry
        sel = (key > v) | (eq & (prefix <= need))
        bias_ref[kc] = jnp.where(sel, 0.0, MASKED)
        return carry + jnp.sum(eqf, axis=-1, keepdims=True)

    lax.fori_loop(0, n_kc, bias_body, jnp.zeros((tq, 1), F32))

    def attn_body(kc, carry):
        m, l, acc = carry
        k0 = pl.multiple_of(kc * ck, ck)
        kb = kk_ref[pl.ds(k0, ck), 0:LANES]
        vb = vv_ref[pl.ds(k0, ck), :]
        s = _dot_nt(q8_ref[...], kb)
        s = (s.reshape(HEADS, tq, ck) + bias_ref[kc][None]).reshape(HEADS * tq, ck)
        m_new = jnp.maximum(m, jnp.max(s, axis=-1, keepdims=True))
        alpha = jnp.exp(m - m_new)
        pexp = jnp.exp(s - m_new)
        l = alpha * l + jnp.sum(pexp, axis=-1, keepdims=True)
        acc = alpha * acc + _dot(pexp.astype(BF16), vb)
        return m_new, l, acc

    init = (jnp.full((HEADS * tq, 1), M_INIT, F32), jnp.zeros((HEADS * tq, 1), F32),
            jnp.zeros((HEADS * tq, LANES), F32))
    _, l, acc = lax.fori_loop(0, n_kc, attn_body, init)
    o = acc / l
    for p in range(HEADS // 2):
        even = o[(2 * p) * tq:(2 * p + 1) * tq, :]
        odd = o[(2 * p + 1) * tq:(2 * p + 2) * tq, :]
        y_ref[:, p * LANES:(p + 1) * LANES] = jnp.where(lo_half, even, odd).astype(BF16)


def _dsa(z, zs, batch, seq, tq, ck):
    nq = seq // tq
    T = batch * seq
    topk = min(DSA_TOPK_MAX, seq // 4)
    triu = jnp.asarray(np.triu(np.ones((ck, ck), np.float32)), BF16)
    return pl.pallas_call(
        functools.partial(_dsa_kernel, tq=tq, ck=ck, topk=topk),
        grid=(batch, nq),
        in_specs=[
            pl.BlockSpec((tq, 2 * LANES), lambda b, i: (b * nq + i, CB_QI // 2)),
            pl.BlockSpec((seq, 2 * LANES), lambda b, i: (b, CB_KB // 2)),
            pl.BlockSpec((seq, LANES), lambda b, i: (b, CB_VB)),
            pl.BlockSpec((tq, BRANCH_WIDTH), lambda b, i: (b * nq + i, CB_QB // 4)),
            pl.BlockSpec((tq, LANES), lambda b, i: (b * nq + i, 0)),
            pl.BlockSpec((ck, ck), lambda b, i: (0, 0)),
        ],
        out_specs=pl.BlockSpec((tq, BRANCH_WIDTH), lambda b, i: (b * nq + i, 0)),
        out_shape=jax.ShapeDtypeStruct((T, BRANCH_WIDTH), BF16),
        scratch_shapes=[
            pltpu.VMEM((seq // ck, tq, ck), I32),
            pltpu.VMEM((seq // ck, tq, ck), F32),
            pltpu.VMEM((IDX_HEADS * tq, LANES), BF16),
            pltpu.VMEM((HEADS * tq, LANES), BF16),
        ],
        compiler_params=pltpu.CompilerParams(dimension_semantics=("parallel", "arbitrary"), vmem_limit_bytes=VMEM_LIMIT),
        name="dsa",
    )(z, z, z, z, zs, triu)


def _ret_kernel(q_ref, k_ref, v_ref, g_ref, dm_ref, xi_ref, zeta_ref, decay_ref, gain_ref, y_ref, r_ref):
    C = RET_CHUNK
    n_chunks = q_ref.shape[0] // C
    lo_half = _lane_lt_half_tile((C, LANES))
    r_ref[...] = jnp.zeros_like(r_ref)
    decay = decay_ref[0]
    block_diag = jnp.where(decay > 0.0, 1.0, 0.0)

    def head_stat(x):
        s_lo = jnp.sum(jnp.where(lo_half, x, 0.0), axis=-1, keepdims=True)
        s_hi = jnp.sum(jnp.where(lo_half, 0.0, x), axis=-1, keepdims=True)
        return jnp.where(lo_half, s_lo, s_hi) * (1.0 / HEAD_DIM)

    def body(c, _):
        rows = pl.ds(pl.multiple_of(c * C, C), C)
        q2, k2, v2 = q_ref[rows, :], k_ref[rows, :], v_ref[rows, :]
        zero = jnp.zeros_like(q2)
        inner = []
        for e in range(2):
            qe = jnp.where(lo_half, q2, zero) if e == 0 else jnp.where(lo_half, zero, q2)
            a = _dot_nt(qe, k2) * dm_ref[e]
            inner.append(_dot(a.astype(BF16), v2))
        state = r_ref[...]
        cross = _dot(q2, state.astype(BF16)) * xi_ref[0]
        r = jnp.where(lo_half, inner[0], inner[1]) + cross
        kz = (k2.astype(F32) * zeta_ref[0]).T.astype(BF16)
        r_ref[...] = state * decay + _dot(kz, v2) * block_diag
        mu = head_stat(r)
        d = r - mu
        var = head_stat(d * d)
        y = d * lax.rsqrt(var + EPS) * gain_ref[...]
        gate = g_ref[rows, :].astype(F32)
        y_ref[rows, :] = (y * (gate * jax.nn.sigmoid(gate))).astype(BF16)
        return 0

    lax.fori_loop(0, n_chunks, body, 0)


def _retention_tables():
    C = RET_CHUNK
    log_g = np.log(1.0 - 2.0 ** (-5.0 - np.arange(HEADS, dtype=np.float64)))
    n = np.arange(C, dtype=np.float64)
    diff = n[:, None] - n[None, :]
    dm = np.where(diff[None] >= 0, np.exp(diff[None] * log_g[:, None, None]), 0.0)
    xi = np.exp((n + 1.0)[None, :] * log_g[:, None])
    zeta = np.exp((C - 1.0 - n)[None, :] * log_g[:, None])
    g_c = np.exp(C * log_g)
    pair = lambda t: np.repeat(t.reshape(HEADS // 2, 2, C), HEAD_DIM, axis=1).transpose(0, 2, 1)
    decay = np.zeros((HEADS // 2, LANES, LANES))
    for p in range(HEADS // 2):
        decay[p, :HEAD_DIM, :HEAD_DIM] = g_c[2 * p]
        decay[p, HEAD_DIM:, HEAD_DIM:] = g_c[2 * p + 1]
    f = lambda t: jnp.asarray(t.astype(np.float32))
    return f(dm), f(pair(xi)), f(pair(zeta)), f(decay)


def _retention(z, gain_row, batch, seq):
    T = batch * seq
    C = RET_CHUNK
    dm, xi, zeta, decay = _retention_tables()
    pairs = HEADS // 2
    return pl.pallas_call(
        _ret_kernel,
        grid=(batch, pairs),
        in_specs=[
            pl.BlockSpec((seq, LANES), lambda b, p: (b, CB_QC + p)),
            pl.BlockSpec((seq, LANES), lambda b, p: (b, CB_KC + p)),
            pl.BlockSpec((seq, LANES), lambda b, p: (b, CB_VC + p)),
            pl.BlockSpec((seq, LANES), lambda b, p: (b, CB_GC + p)),
            pl.BlockSpec((2, C, C), lambda b, p: (p, 0, 0)),
            pl.BlockSpec((1, C, LANES), lambda b, p: (p, 0, 0)),
            pl.BlockSpec((1, C, LANES), lambda b, p: (p, 0, 0)),
            pl.BlockSpec((1, LANES, LANES), lambda b, p: (p, 0, 0)),
            pl.BlockSpec((1, LANES), lambda b, p: (0, p)),
        ],
        out_specs=pl.BlockSpec((seq, LANES), lambda b, p: (b, p)),
        out_shape=jax.ShapeDtypeStruct((T, BRANCH_WIDTH), BF16),
        scratch_shapes=[pltpu.VMEM((LANES, LANES), F32)],
        compiler_params=pltpu.CompilerParams(dimension_semantics=("parallel", "parallel")),
        name="retention",
    )(z, z, z, z, dm, xi, zeta, decay, gain_row)


def _swa_kernel(q_ref, kc_ref, kp_ref, vc_ref, vp_ref, sink_ref, y_ref):
    n = pl.program_id(1)
    W = SWA_WINDOW
    G = HEADS // SWA_KV_HEADS
    lo_half = _lane_lt_half_tile((W, LANES))
    qi = lax.broadcasted_iota(I32, (W, 2 * W), 0)
    kj = lax.broadcasted_iota(I32, (W, 2 * W), 1)
    band = (kj > qi) & (kj <= qi + W) & ((kj >= W) | (n > 0))
    for g in range(SWA_KV_HEADS):
        cols = slice(g * LANES, (g + 1) * LANES)
        kband = jnp.concatenate([kp_ref[:, cols], kc_ref[:, cols]], axis=0)
        vband = jnp.concatenate([vp_ref[:, cols], vc_ref[:, cols]], axis=0)
        q_rows, sink_rows = [], []
        for r in range(G):
            h = G * g + r
            t = q_ref[:, (h // 2) * LANES:(h // 2 + 1) * LANES]
            keep = lo_half if h % 2 == 0 else jnp.logical_not(lo_half)
            q_rows.append(jnp.where(keep, t, jnp.zeros_like(t)))
            sink_rows.append(jnp.broadcast_to(sink_ref[:, h:h + 1], (W, 1)))
        q4 = jnp.concatenate(q_rows, axis=0)
        sink = jnp.concatenate(sink_rows, axis=0)
        s = _dot_nt(q4, kband).reshape(G, W, 2 * W)
        s = jnp.where(band[None], s, MASKED).reshape(G * W, 2 * W)
        m = jnp.maximum(jnp.max(s, axis=-1, keepdims=True), sink)
        pexp = jnp.exp(s - m)
        denom = jnp.sum(pexp, axis=-1, keepdims=True) + jnp.exp(sink - m)
        o = _dot((pexp / denom).astype(BF16), vband)
        for r2 in range(G // 2):
            even = o[(2 * r2) * W:(2 * r2 + 1) * W, :]
            odd = o[(2 * r2 + 1) * W:(2 * r2 + 2) * W, :]
            pcol = (G * g) // 2 + r2
            y_ref[:, pcol * LANES:(pcol + 1) * LANES] = jnp.where(lo_half, even, odd).astype(BF16)


def _swa(z, sink_row, batch, seq):
    T = batch * seq
    W = SWA_WINDOW
    nb = seq // W
    cur = lambda cb: (lambda b, n: (b * nb + n, cb))
    prev = lambda cb: (lambda b, n: (b * nb + jnp.maximum(n - 1, 0), cb))
    return pl.pallas_call(
        _swa_kernel,
        grid=(batch, nb),
        in_specs=[
            pl.BlockSpec((W, BRANCH_WIDTH), cur(CB_QD // 4)),
            pl.BlockSpec((W, 2 * LANES), cur(CB_KD // 2)),
            pl.BlockSpec((W, 2 * LANES), prev(CB_KD // 2)),
            pl.BlockSpec((W, 2 * LANES), cur(CB_VD // 2)),
            pl.BlockSpec((W, 2 * LANES), prev(CB_VD // 2)),
            pl.BlockSpec((1, LANES), lambda b, n: (0, 0)),
        ],
        out_specs=pl.BlockSpec((W, BRANCH_WIDTH), lambda b, n: (b * nb + n, 0)),
        out_shape=jax.ShapeDtypeStruct((T, BRANCH_WIDTH), BF16),
        compiler_params=pltpu.CompilerParams(dimension_semantics=("parallel", "arbitrary")),
        name="swa",
    )(z, z, z, z, z, sink_row)


def _merge_kernel(x_ref, g_ref, ya_ref, yb_ref, yc_ref, yd_ref, wg_ref, wb_ref, wo_ref, o_ref):
    x = x_ref[...]
    D = x.shape[1]
    hb = _rmsnorm_f32(x, g_ref[...]).astype(BF16)
    merged = jnp.zeros(x.shape, F32)
    for n, y_ref in enumerate((ya_ref, yb_ref, yc_ref, yd_ref)):
        gate = jax.nn.sigmoid(_dot(hb, wg_ref[:, n * D:(n + 1) * D]))
        merged = merged + gate * _dot(y_ref[...], wb_ref[n])
    o_ref[...] = x + _dot(merged.astype(BF16), wo_ref[...])


def _merge(x2, g, ys, w_gates, w_branch, w_out):
    T, D = x2.shape
    tm = min(512, T)
    const = lambda shape: pl.BlockSpec(shape, lambda i: (0,) * len(shape), pipeline_mode=pl.Buffered(1))
    y_spec = pl.BlockSpec((tm, BRANCH_WIDTH), lambda i: (i, 0))
    return pl.pallas_call(
        _merge_kernel,
        grid=(T // tm,),
        in_specs=[
            pl.BlockSpec((tm, D), lambda i: (i, 0)),
            const((1, D)),
            y_spec, y_spec, y_spec, y_spec,
            const((D, N_BRANCH * D)),
            const((N_BRANCH, BRANCH_WIDTH, D)),
            const((D, D)),
        ],
        out_specs=pl.BlockSpec((tm, D), lambda i: (i, 0)),
        out_shape=jax.ShapeDtypeStruct((T, D), F32),
        compiler_params=pltpu.CompilerParams(dimension_semantics=("parallel",), vmem_limit_bytes=VMEM_LIMIT),
        name="merge",
    )(x2, g, *ys, w_gates, w_branch, w_out)


def _ffn_kernel(x_ref, g_ref, wup_ref, cw_ref, cb_ref, wdn_ref, fg_ref, o_ref, carry_ref, u_ref, *, tc, final_norm):
    n = pl.program_id(1)
    tm = x_ref.shape[0]
    d_ff = wdn_ref.shape[0]
    PAD = 8

    @pl.when(n == 0)
    def _():
        carry_ref[...] = jnp.zeros_like(carry_ref)

    x = x_ref[...]
    hb = _rmsnorm_f32(x, g_ref[...]).astype(BF16)

    def conv(col0):
        cols = slice(col0, col0 + tc)
        u = _dot(hb, wup_ref[:, cols])
        u_ref[0:PAD, :] = carry_ref[:, cols]
        u_ref[PAD:PAD + tm, :] = u
        carry_ref[:, cols] = u[tm - PAD:tm, :]
        w = cw_ref[:, cols]
        return (w[2:3, :] * u + w[1:2, :] * u_ref[PAD - 1:PAD - 1 + tm, :]
                + w[0:1, :] * u_ref[PAD - 2:PAD - 2 + tm, :] + cb_ref[:, cols])

    acc = jnp.zeros(x.shape, F32)
    for c in range(d_ff // tc):
        a = conv(c * tc)
        b = conv(d_ff + c * tc)
        act = (a * jax.nn.sigmoid(a) * b).astype(BF16)
        acc = acc + _dot(act, wdn_ref[c * tc:(c + 1) * tc, :])
    y = x + acc
    if final_norm:
        y = _rmsnorm_f32(y, fg_ref[...])
    o_ref[...] = y


def _ffn(x2, g, w_up, conv_w, conv_b, w_down, final_g, batch, seq, final_norm):
    T, D = x2.shape
    d_ff = w_down.shape[0]
    tm = min(512, seq)
    tc = 256
    nb = seq // tm
    const = lambda shape: pl.BlockSpec(shape, lambda b, n: (0,) * len(shape), pipeline_mode=pl.Buffered(1))
    return pl.pallas_call(
        functools.partial(_ffn_kernel, tc=tc, final_norm=final_norm),
        grid=(batch, nb),
        in_specs=[
            pl.BlockSpec((tm, D), lambda b, n: (b * nb + n, 0)),
            const((1, D)),
            const((D, 2 * d_ff)),
            const((CONV_WIDTH, 2 * d_ff)),
            const((1, 2 * d_ff)),
            const((d_ff, D)),
            const((1, D)),
        ],
        out_specs=pl.BlockSpec((tm, D), lambda b, n: (b * nb + n, 0)),
        out_shape=jax.ShapeDtypeStruct((T, D), F32),
        scratch_shapes=[pltpu.VMEM((8, 2 * d_ff), F32), pltpu.VMEM((tm + 8, tc), F32)],
        compiler_params=pltpu.CompilerParams(dimension_semantics=("parallel", "arbitrary"), vmem_limit_bytes=VMEM_LIMIT),
        name="ffn",
    )(x2, g, w_up, conv_w, conv_b, w_down, final_g)


def _rope_tables(seq):
    pos = jnp.arange(seq, dtype=F32)
    inv = 1.0 / (ROPE_THETA ** (jnp.arange(HALF, dtype=F32) * 2.0 / HEAD_DIM))
    ang = pos[:, None] * inv[None, :]
    cos, sin = jnp.cos(ang), jnp.sin(ang)
    cos_t = jnp.tile(cos, (1, LANES // HALF))
    sin_t = jnp.tile(jnp.concatenate([-sin, sin], axis=1), (1, LANES // HEAD_DIM))
    return cos_t, sin_t


def kernel(x, attn_norm, w_in, forget_bias, ret_norm, attn_sinks, w_branch, w_out, ffn_norm, w_up, conv_w, conv_b, w_down, final_norm):
    B, S, D = x.shape
    depth = w_in.shape[0]
    T = B * S
    perm, scale = _column_plan()
    scale_row = jnp.asarray(scale)[None, :]
    cos_t, sin_t = _rope_tables(S)
    tq_fox = tk_fox = min(256, S)
    tq_dsa, ck_dsa = 128, min(256, S)
    pad_row = lambda v: jnp.pad(v.astype(F32), (0, LANES - v.shape[0]))[None, :]

    x2 = x.reshape(T, D)
    for l in range(depth):
        w_l = jnp.concatenate([w_in[l], jnp.zeros((D, 1), w_in.dtype)], axis=1)
        w_main = jnp.take(w_l, perm, axis=1).astype(BF16)
        w_small = jnp.concatenate([w_in[l][:, O_FA:O_FA + HEADS], w_in[l][:, O_WI:O_WI + IDX_HEADS],
                                   jnp.zeros((D, LANES - HEADS - IDX_HEADS), w_in.dtype)], axis=1).astype(BF16)
        w_gates = w_in[l][:, O_GATES:].astype(BF16)

        z, zs = _inproj(x2, attn_norm[l][None, :], w_main, w_small, cos_t, sin_t, scale_row, S)
        fcol = _fcum(zs, pad_row(forget_bias[l]), S)
        frow = fcol.reshape(B, S, LANES)[:, :, :HEADS].transpose(0, 2, 1).reshape(B, HEADS, S // tk_fox, tk_fox)

        y_a = _fox(z, fcol, frow, B, S, tq_fox, tk_fox)
        y_b = _dsa(z, zs, B, S, tq_dsa, ck_dsa)
        y_c = _retention(z, ret_norm[l][None, :].astype(F32), B, S)
        y_d = _swa(z, pad_row(attn_sinks[l]), B, S)

        x2 = _merge(x2, attn_norm[l][None, :], (y_a, y_b, y_c, y_d), w_gates,
                    w_branch[l].astype(BF16), w_out[l].astype(BF16))
        x2 = _ffn(x2, ffn_norm[l][None, :], w_up[l].astype(BF16), conv_w[l], conv_b[l][None, :],
                  w_down[l].astype(BF16), final_norm[None, :], B, S, final_norm=(l == depth - 1))
    return x2.reshape(B, S, D)
```

```python
import functools

import numpy as np
import jax
import jax.numpy as jnp
from jax import lax
from jax.experimental import pallas as pl
from jax.experimental.pallas import tpu as pltpu

F32 = jnp.float32
BF16 = jnp.bfloat16
I32 = jnp.int32
I16 = jnp.int16
I16_MIN = -2 ** 15
I16_ROWS = 16

LANES = 128
HEAD_DIM = 64
HALF = HEAD_DIM // 2
HEADS = 8
BRANCH_WIDTH = HEADS * HEAD_DIM
IDX_HEADS = 4
SWA_KV_HEADS = 2
SWA_WINDOW = 128
RET_CHUNK = 128
N_BRANCH = 4
CONV_WIDTH = 3
DSA_TOPK_MAX = 256
ROPE_THETA = 10000.0
EPS = 1e-6
VMEM_LIMIT = 56 * 1024 * 1024

VT_ROWS = HEAD_DIM + 16
F_PIECES = 3
M_INIT = -1e30
MASKED = -2e30
INT_MIN = -2 ** 31

CB_QA, CB_KA, CB_VA, CB_VC, CB_GC = 0, 4, 8, 12, 16
CB_VB, CB_VD = 20, 22
N_PLAIN_BLOCKS = 24
CB_QB, CB_QC, CB_KC, CB_QD, CB_QI = 24, 28, 32, 36, 40
CB_KB, CB_KI, CB_KD = 42, 43, 44
N_BLOCKS = 46
TN_IN = 256

_SIZES = (512, 512, 512, 8, 512, 64, 64, 256, 64, 4, 512, 512, 512, 512, 512, 128, 128, 4096)
_OFFS = np.concatenate([[0], np.cumsum(_SIZES)]).tolist()
(O_QA, O_KA, O_VA, O_FA, O_QB, O_KB, O_VB, O_QI, O_KI, O_WI,
 O_QC, O_KC, O_VC, O_GC, O_QD, O_KD, O_VD, O_GATES, N_IN) = _OFFS
ZS_WI = 8


def _column_plan():
    perm = np.full((N_BLOCKS * LANES,), N_IN, np.int32)
    scale = np.ones((N_BLOCKS * LANES,), np.float32)

    def put(block, off, width, s=1.0):
        perm[block * LANES: block * LANES + width] = np.arange(off, off + width)
        scale[block * LANES: block * LANES + width] = s

    q_scale = HEAD_DIM ** -0.5
    put(CB_QA, O_QA, 512, q_scale); put(CB_KA, O_KA, 512); put(CB_VA, O_VA, 512)
    put(CB_VC, O_VC, 512); put(CB_GC, O_GC, 512)
    for half in range(2):
        perm[CB_VB * LANES + half * 64: CB_VB * LANES + half * 64 + 64] = np.arange(O_VB, O_VB + 64)
        perm[CB_KB * LANES + half * 64: CB_KB * LANES + half * 64 + 64] = np.arange(O_KB, O_KB + 64)
        perm[CB_KI * LANES + half * 64: CB_KI * LANES + half * 64 + 64] = np.arange(O_KI, O_KI + 64)
        for g in range(SWA_KV_HEADS):
            perm[(CB_VD + g) * LANES + half * 64: (CB_VD + g) * LANES + half * 64 + 64] = np.arange(O_VD + 64 * g, O_VD + 64 * g + 64)
            perm[(CB_KD + g) * LANES + half * 64: (CB_KD + g) * LANES + half * 64 + 64] = np.arange(O_KD + 64 * g, O_KD + 64 * g + 64)
    put(CB_QB, O_QB, 512, q_scale); put(CB_QC, O_QC, 512); put(CB_KC, O_KC, 512, q_scale)
    put(CB_QD, O_QD, 512, q_scale); put(CB_QI, O_QI, 256)
    return perm, scale


def _segment_offsets(perm, scale):
    offs, n = [], 0
    for _, width in _column_segments(perm, scale):
        offs.append(n)
        n += width
    return offs


def _column_segments(perm, scale):
    segments, n = [], 0
    while n < len(perm):
        e = n + 1
        if perm[n] == N_IN:
            while e < len(perm) and perm[e] == N_IN:
                e += 1
            segments.append((None, e - n))
        else:
            while e < len(perm) and perm[e] == perm[e - 1] + 1 and scale[e] == scale[n]:
                e += 1
            segments.append((int(perm[n]), e - n))
        n = e
    return segments


def _rmsnorm_f32(x, g):
    ms = jnp.mean(x * x, axis=-1, keepdims=True)
    return x * lax.rsqrt(ms + EPS) * g


def _lane_lt_half_tile(shape):
    return lax.broadcasted_iota(I32, shape, len(shape) - 1) < HEAD_DIM


def _dot_nt(a, b):
    return lax.dot_general(a, b, (((1,), (1,)), ((), ())), preferred_element_type=F32)


def _dot(a, b):
    return jnp.dot(a, b, preferred_element_type=F32)


def _repack_kernel(w_ref, main_ref, small_ref, gates_ref, *, segments):
    rows = w_ref.shape[0]
    off = 0
    for start, width, s in segments:
        if start is None:
            main_ref[:, off:off + width] = jnp.zeros((rows, width), BF16)
        else:
            main_ref[:, off:off + width] = (w_ref[:, start:start + width] * s).astype(BF16)
        off += width
    small_ref[:, 0:HEADS] = w_ref[:, O_FA:O_FA + HEADS].astype(BF16)
    small_ref[:, ZS_WI:ZS_WI + IDX_HEADS] = w_ref[:, O_WI:O_WI + IDX_HEADS].astype(BF16)
    small_ref[:, ZS_WI + IDX_HEADS:] = jnp.zeros((rows, LANES - ZS_WI - IDX_HEADS), BF16)
    gates_ref[...] = w_ref[:, O_GATES:N_IN].astype(BF16)


def _repack_w_in(w_in, layer):
    _, D, n_in = w_in.shape
    perm, scale = _column_plan()
    segments = tuple((start, width, float(scale[off]))
                     for (start, width), off in zip(_column_segments(perm, scale), _segment_offsets(perm, scale)))
    tr = 128
    n_main = N_BLOCKS * LANES
    n_gates = N_IN - O_GATES
    return pl.pallas_call(
        functools.partial(_repack_kernel, segments=segments),
        grid=(D // tr,),
        in_specs=[pl.BlockSpec((None, tr, n_in), lambda r: (layer, r, 0))],
        out_specs=[pl.BlockSpec((tr, n_main), lambda r: (r, 0)), pl.BlockSpec((tr, LANES), lambda r: (r, 0)),
                   pl.BlockSpec((tr, n_gates), lambda r: (r, 0))],
        out_shape=[jax.ShapeDtypeStruct((D, n_main), BF16), jax.ShapeDtypeStruct((D, LANES), BF16),
                   jax.ShapeDtypeStruct((D, n_gates), BF16)],
        compiler_params=pltpu.CompilerParams(dimension_semantics=("parallel",), vmem_limit_bytes=VMEM_LIMIT),
        name="repack",
    )(w_in)


def _inproj_kernel(x_ref, g_ref, w_ref, ws_ref, cos_ref, sin_ref, z_ref, zs_ref, h_ref, *, n_plain_tiles):
    j = pl.program_id(1)

    @pl.when(j == 0)
    def _():
        hb = _rmsnorm_f32(x_ref[...], g_ref[...]).astype(BF16)
        h_ref[...] = hb
        zs_ref[...] = _dot(hb, ws_ref[...])

    acc = _dot(h_ref[...], w_ref[...])

    @pl.when(j < n_plain_tiles)
    def _():
        z_ref[...] = acc.astype(BF16)

    @pl.when(j >= n_plain_tiles)
    def _():
        tm = acc.shape[0]
        lane = lax.broadcasted_iota(I32, (tm, LANES), 1)
        first_half = (lane % HEAD_DIM) < HALF
        for c in range(acc.shape[1] // LANES):
            a = acc[:, c * LANES:(c + 1) * LANES]
            rot = jnp.where(first_half, pltpu.roll(a, LANES - HALF, 1), pltpu.roll(a, HALF, 1))
            z_ref[:, c * LANES:(c + 1) * LANES] = (a * cos_ref[...] + rot * sin_ref[...]).astype(BF16)


def _inproj(x2, g, w_main, w_small, cos_t, sin_t, seq):
    T, D = x2.shape
    N = w_main.shape[1]
    tm = min(1024, seq)
    tn = TN_IN
    pos_blocks = seq // tm
    return pl.pallas_call(
        functools.partial(_inproj_kernel, n_plain_tiles=N_PLAIN_BLOCKS * LANES // tn),
        grid=(T // tm, N // tn),
        in_specs=[
            pl.BlockSpec((tm, D), lambda i, j: (i, 0)),
            pl.BlockSpec((1, D), lambda i, j: (0, 0)),
            pl.BlockSpec((D, tn), lambda i, j: (0, j)),
            pl.BlockSpec((D, LANES), lambda i, j: (0, 0)),
            pl.BlockSpec((tm, LANES), lambda i, j: (i % pos_blocks, 0)),
            pl.BlockSpec((tm, LANES), lambda i, j: (i % pos_blocks, 0)),
        ],
        out_specs=[
            pl.BlockSpec((tm, tn), lambda i, j: (i, j)),
            pl.BlockSpec((tm, LANES), lambda i, j: (i, 0)),
        ],
        out_shape=[jax.ShapeDtypeStruct((T, N), BF16), jax.ShapeDtypeStruct((T, LANES), F32)],
        scratch_shapes=[pltpu.VMEM((tm, D), BF16)],
        compiler_params=pltpu.CompilerParams(dimension_semantics=("parallel", "arbitrary"), vmem_limit_bytes=VMEM_LIMIT),
        name="inproj",
    )(x2, g, w_main, w_small, cos_t, sin_t)


def _split3(x):
    hi = x.astype(BF16)
    r1 = x - hi.astype(F32)
    mid = r1.astype(BF16)
    lo = (r1 - mid.astype(F32)).astype(BF16)
    return hi, mid, lo


def _fcum_kernel(zs_ref, bias_ref, tri_ref, place_ref, k_ref, kf_ref):
    n_chunks = zs_ref.shape[0] // LANES
    lo_half = _lane_lt_half_tile((LANES, LANES))

    def body(c, carry):
        rows = pl.ds(pl.multiple_of(c * LANES, LANES), LANES)
        xv = zs_ref[rows, :] + bias_ref[...]
        logf = jnp.minimum(xv, 0.0) - jnp.log(1.0 + jnp.exp(-jnp.abs(xv)))
        tri = tri_ref[...]
        cs = carry
        for piece in _split3(logf):
            cs = cs + _dot(tri, piece)
        placed = jnp.zeros((LANES, HEADS * LANES), F32)
        for j, piece in enumerate(_split3(cs)):
            placed = placed + _dot(piece, place_ref[j])
        for h in range(HEADS):
            kp = k_ref[rows, (h // 2) * LANES:(h // 2 + 1) * LANES]
            own = lo_half if h % 2 == 0 else jnp.logical_not(lo_half)
            kf_ref[rows, h * LANES:(h + 1) * LANES] = jnp.where(own, kp, placed[:, h * LANES:(h + 1) * LANES].astype(BF16))
        return cs[LANES - 1:LANES, :]

    lax.fori_loop(0, n_chunks, body, jnp.zeros((1, LANES), F32))


def _f_piece_row(h):
    return HEAD_DIM if h % 2 == 0 else 0


def _fcum(zs, bias_row, z, seq):
    T = zs.shape[0]
    tri = jnp.asarray(np.tril(np.ones((LANES, LANES), np.float32)), BF16)
    place = np.zeros((F_PIECES, LANES, HEADS * LANES), np.float32)
    for h in range(HEADS):
        for j in range(F_PIECES):
            place[j, h, h * LANES + _f_piece_row(h) + j] = 1.0
    return pl.pallas_call(
        _fcum_kernel,
        grid=(T // seq,),
        in_specs=[
            pl.BlockSpec((seq, LANES), lambda b: (b, 0)),
            pl.BlockSpec((1, LANES), lambda b: (0, 0)),
            pl.BlockSpec((LANES, LANES), lambda b: (0, 0)),
            pl.BlockSpec((F_PIECES, LANES, HEADS * LANES), lambda b: (0, 0, 0)),
            pl.BlockSpec((seq, BRANCH_WIDTH), lambda b: (b, CB_KA // 4)),
        ],
        out_specs=pl.BlockSpec((seq, HEADS * LANES), lambda b: (b, 0)),
        out_shape=jax.ShapeDtypeStruct((T, HEADS * LANES), BF16),
        compiler_params=pltpu.CompilerParams(dimension_semantics=("parallel",), vmem_limit_bytes=VMEM_LIMIT),
        name="fcum",
    )(zs, bias_row, tri, jnp.asarray(place, BF16), z)


def _flash_chunks(n_off, scores, values, sa_ref, sb_ref, m_ref, acc_ref):
    def produce(buf, kc, diagonal, h):
        buf[h] = scores(h, kc, diagonal)

    def consume(buf, kc, h):
        st = buf[h]
        m_old = m_ref[h]
        m_new = jnp.maximum(m_old, jnp.max(st, axis=0, keepdims=True))
        pt = jnp.exp(st - m_new).astype(BF16)
        acc_ref[h] = jnp.exp(m_old - m_new) * acc_ref[h] + _dot(values(h, kc), pt)
        m_ref[h] = m_new

    def consume_and_produce(cbuf, ckc, pbuf, pkc):
        for h in range(HEADS):
            produce(pbuf, pkc, False, h)
            consume(cbuf, ckc, h)

    def chunk_of_item(t):
        return jnp.where(t == 0, n_off, t - 1)

    for h in range(HEADS):
        produce(sa_ref, n_off, True, h)

    def pair(j, carry):
        consume_and_produce(sa_ref, chunk_of_item(2 * j), sb_ref, 2 * j)
        consume_and_produce(sb_ref, 2 * j, sa_ref, 2 * j + 1)
        return carry

    lax.fori_loop(0, n_off // 2, pair, 0)

    @pl.when(n_off % 2 == 0)
    def _():
        for h in range(HEADS):
            consume(sa_ref, chunk_of_item(n_off), h)

    @pl.when(n_off % 2 == 1)
    def _():
        consume_and_produce(sa_ref, chunk_of_item(n_off - 1), sb_ref, n_off - 1)
        for h in range(HEADS):
            consume(sb_ref, n_off - 1, h)


def _fox_kernel(q_ref, k_ref, v_ref, y_ref, vt_ref, qt_ref, sa_ref, sb_ref, m_ref, acc_ref, *, tq, tk):
    i = pl.program_id(1)
    n_kc = k_ref.shape[0] // tk
    pairs = HEADS // 2

    @pl.when(i == 0)
    def _():
        ones_row = jnp.where(lax.broadcasted_iota(I32, (VT_ROWS - HEAD_DIM, tk), 0) == 0, 1.0, 0.0).astype(BF16)
        for p in range(pairs):
            for c in range(n_kc):
                t = v_ref[c * tk:(c + 1) * tk, p * LANES:(p + 1) * LANES].astype(F32).T.astype(BF16)
                for e in range(2):
                    vt_ref[2 * p + e, c, 0:HEAD_DIM, :] = t[HEAD_DIM * e:HEAD_DIM * (e + 1), :]
                    vt_ref[2 * p + e, c, HEAD_DIM:VT_ROWS, :] = ones_row

    row = lax.broadcasted_iota(I32, (LANES, tq), 0)
    for p in range(pairs):
        qt = q_ref[:, p * LANES:(p + 1) * LANES].astype(F32).T
        for e in range(2):
            own = (row < HEAD_DIM) if e == 0 else (row >= HEAD_DIM)
            f0 = _f_piece_row(e)
            minus_one = jnp.where((row >= f0) & (row < f0 + F_PIECES), -1.0, 0.0)
            qt_ref[2 * p + e] = jnp.where(own, qt, minus_one).astype(BF16)
    m_ref[...] = jnp.full(m_ref.shape, M_INIT, F32)
    acc_ref[...] = jnp.zeros(acc_ref.shape, F32)

    def scores(h, kc, diagonal):
        k0 = pl.multiple_of(kc * tk, tk)
        st = _dot(k_ref[pl.ds(k0, tk), h * LANES:(h + 1) * LANES], qt_ref[h])
        if diagonal:
            key_pos = k0 + lax.broadcasted_iota(I32, (tk, 1), 0)
            query_pos = i * tq + lax.broadcasted_iota(I32, (1, tq), 1)
            st = jnp.where(key_pos <= query_pos, st, MASKED)
        return st

    _flash_chunks(i, scores, lambda h, kc: vt_ref[h, kc], sa_ref, sb_ref, m_ref, acc_ref)

    for p in range(pairs):
        heads = []
        for h in (2 * p, 2 * p + 1):
            heads.append(acc_ref[h, 0:HEAD_DIM, :] / acc_ref[h, HEAD_DIM:HEAD_DIM + 1, :])
        y_ref[:, p * LANES:(p + 1) * LANES] = jnp.concatenate(heads, axis=0).T.astype(BF16)


def _fox(z, kf, batch, seq, tile):
    nq = seq // tile
    T = batch * seq
    return pl.pallas_call(
        functools.partial(_fox_kernel, tq=tile, tk=tile),
        grid=(batch, nq),
        in_specs=[
            pl.BlockSpec((tile, BRANCH_WIDTH), lambda b, i: (b * nq + i, CB_QA // 4)),
            pl.BlockSpec((seq, HEADS * LANES), lambda b, i: (b, 0)),
            pl.BlockSpec((seq, BRANCH_WIDTH), lambda b, i: (b, CB_VA // 4)),
        ],
        out_specs=pl.BlockSpec((tile, BRANCH_WIDTH), lambda b, i: (b * nq + i, 0)),
        out_shape=jax.ShapeDtypeStruct((T, BRANCH_WIDTH), BF16),
        scratch_shapes=[
            pltpu.VMEM((HEADS, seq // tile, VT_ROWS, tile), BF16),
            pltpu.VMEM((HEADS, LANES, tile), BF16),
            pltpu.VMEM((HEADS, tile, tile), F32),
            pltpu.VMEM((HEADS, tile, tile), F32),
            pltpu.VMEM((HEADS, 1, tile), F32),
            pltpu.VMEM((HEADS, VT_ROWS, tile), F32),
        ],
        compiler_params=pltpu.CompilerParams(dimension_semantics=("parallel", "arbitrary"), vmem_limit_bytes=VMEM_LIMIT),
        name="fox",
    )(z, kf, z)


def _half_rows_operand(qt, first_half):
    row = lax.broadcasted_iota(I32, qt.shape, 0)
    own = (row < HEAD_DIM) if first_half else (row >= HEAD_DIM)
    return jnp.where(own, qt, 0.0).astype(BF16)


def _count16(ref, n_chunks, pred):
    ck, tq = ref.shape[1:]

    def body(kc, acc):
        hit = jnp.where(pred(ref[kc]), jnp.int16(1), jnp.int16(0))
        for r in range(ck // I16_ROWS):
            acc = acc + hit[r * I16_ROWS:(r + 1) * I16_ROWS, :]
        return acc

    acc = lax.fori_loop(0, n_chunks, body, jnp.zeros((I16_ROWS, tq), I16))
    return jnp.sum(acc.astype(I32), axis=0, keepdims=True)


def _search16(ref, n_chunks, want):
    v0 = jnp.where(_count16(ref, n_chunks, lambda k: k >= 0) >= want, 0, I16_MIN).astype(I32)

    def bit_body(b, v):
        cand = v | jnp.left_shift(jnp.int32(1), 14 - b)
        c = _count16(ref, n_chunks, lambda k: k >= cand.astype(I16))
        return jnp.where(c >= want, cand, v)

    return lax.fori_loop(0, 15, bit_body, v0)


def _topk_threshold(key_ref, hi_ref, lo_ref, n_chunks, topk):
    hi_v = _search16(hi_ref, n_chunks, topk)
    want_lo = topk - _count16(hi_ref, n_chunks, lambda k: k > hi_v.astype(I16))

    def fill_lo(kc, carry):
        key = key_ref[kc]
        lo = (key & 0xFFFF) - 32768
        lo_ref[kc] = jnp.where((key >> 16) == hi_v, lo, I16_MIN).astype(I16)
        return carry

    lax.fori_loop(0, n_chunks, fill_lo, 0)
    lo_v = _search16(lo_ref, n_chunks, want_lo)
    need = want_lo - _count16(lo_ref, n_chunks, lambda k: k > lo_v.astype(I16))
    v = hi_v * 65536 + (lo_v + 32768)
    return v, need.astype(F32)


def _dsa_kernel(qi_ref, kk_ref, vv_ref, qb_ref, zs_ref, tril_ref, y_ref,
                key_ref, hi_ref, lo_ref, bias_ref, vt_ref, qit_ref, qbt_ref, sa_ref, sb_ref, m_ref, acc_ref, *, tq, ck, topk):
    i = pl.program_id(1)
    n_chunks = i + 1
    n_kc_total = kk_ref.shape[0] // ck

    @pl.when(i == 0)
    def _():
        ones_row = jnp.where(lax.broadcasted_iota(I32, (VT_ROWS - HEAD_DIM, ck), 0) == 0, 1.0, 0.0).astype(BF16)
        for c in range(n_kc_total):
            t = vv_ref[c * ck:(c + 1) * ck, :].astype(F32).T.astype(BF16)
            vt_ref[c, 0:HEAD_DIM, :] = t[0:HEAD_DIM, :]
            vt_ref[c, HEAD_DIM:VT_ROWS, :] = ones_row

    for p in range(IDX_HEADS // 2):
        qt = qi_ref[:, p * LANES:(p + 1) * LANES].astype(F32).T
        qit_ref[2 * p] = _half_rows_operand(qt, True)
        qit_ref[2 * p + 1] = _half_rows_operand(qt, False)
    for p in range(HEADS // 2):
        qt = qb_ref[:, p * LANES:(p + 1) * LANES].astype(F32).T
        qbt_ref[2 * p] = _half_rows_operand(qt, True)
        qbt_ref[2 * p + 1] = _half_rows_operand(qt, False)
    w_scale = (IDX_HEADS ** -0.5) * (HEAD_DIM ** -0.5)
    w_rows = zs_ref[...].T[ZS_WI:ZS_WI + IDX_HEADS, :] * w_scale

    def score_chunk(kc, diagonal):
        k0 = pl.multiple_of(kc * ck, ck)
        ki = kk_ref[pl.ds(k0, ck), LANES:2 * LANES]
        s = jnp.zeros((ck, tq), F32)
        for h in range(IDX_HEADS):
            s = s + jnp.maximum(_dot(ki, qit_ref[h]), 0.0) * w_rows[h:h + 1, :]
        s = jnp.where(s == 0.0, 0.0, s)
        bits = pltpu.bitcast(s, I32)
        key = bits ^ ((bits >> 31) & 0x7FFFFFFF)
        if diagonal:
            key_pos = k0 + lax.broadcasted_iota(I32, (ck, 1), 0)
            query_pos = i * tq + lax.broadcasted_iota(I32, (1, tq), 1)
            key = jnp.where(key_pos <= query_pos, key, INT_MIN)
        key_ref[kc] = key
        hi_ref[kc] = (key >> 16).astype(I16)

    def score_body(kc, carry):
        score_chunk(kc, False)
        return carry

    lax.fori_loop(0, i, score_body, 0)
    score_chunk(i, True)

    v, need = _topk_threshold(key_ref, hi_ref, lo_ref, n_chunks, topk)

    def bias_body(kc, carry):
        key = key_ref[kc]
        eq = (key == v) & (key > INT_MIN)
        eqf = jnp.where(eq, 1.0, 0.0)
        prefix = _dot(tril_ref[...], eqf.astype(BF16)) + carry
        sel = (key > v) | (eq & (prefix <= need))
        bias_ref[kc] = jnp.where(sel, 0.0, MASKED)
        return carry + jnp.sum(eqf, axis=0, keepdims=True)

    lax.fori_loop(0, n_chunks, bias_body, jnp.zeros((1, tq), F32))

    m_ref[...] = jnp.full(m_ref.shape, M_INIT, F32)
    acc_ref[...] = jnp.zeros(acc_ref.shape, F32)

    def scores(h, kc, diagonal):
        k0 = pl.multiple_of(kc * ck, ck)
        return _dot(kk_ref[pl.ds(k0, ck), 0:LANES], qbt_ref[h]) + bias_ref[kc]

    _flash_chunks(i, scores, lambda h, kc: vt_ref[kc], sa_ref, sb_ref, m_ref, acc_ref)

    for p in range(HEADS // 2):
        heads = []
        for h in (2 * p, 2 * p + 1):
            heads.append(acc_ref[h, 0:HEAD_DIM, :] / acc_ref[h, HEAD_DIM:HEAD_DIM + 1, :])
        y_ref[:, p * LANES:(p + 1) * LANES] = jnp.concatenate(heads, axis=0).T.astype(BF16)


def _dsa(z, zs, batch, seq, tile):
    nq = seq // tile
    T = batch * seq
    topk = min(DSA_TOPK_MAX, seq // 4)
    tril = jnp.asarray(np.tril(np.ones((tile, tile), np.float32)), BF16)
    return pl.pallas_call(
        functools.partial(_dsa_kernel, tq=tile, ck=tile, topk=topk),
        grid=(batch, nq),
        in_specs=[
            pl.BlockSpec((tile, 2 * LANES), lambda b, i: (b * nq + i, CB_QI // 2)),
            pl.BlockSpec((seq, 2 * LANES), lambda b, i: (b, CB_KB // 2)),
            pl.BlockSpec((seq, LANES), lambda b, i: (b, CB_VB)),
            pl.BlockSpec((tile, BRANCH_WIDTH), lambda b, i: (b * nq + i, CB_QB // 4)),
            pl.BlockSpec((tile, LANES), lambda b, i: (b * nq + i, 0)),
            pl.BlockSpec((tile, tile), lambda b, i: (0, 0)),
        ],
        out_specs=pl.BlockSpec((tile, BRANCH_WIDTH), lambda b, i: (b * nq + i, 0)),
        out_shape=jax.ShapeDtypeStruct((T, BRANCH_WIDTH), BF16),
        scratch_shapes=[
            pltpu.VMEM((seq // tile, tile, tile), I32),
            pltpu.VMEM((seq // tile, tile, tile), I16),
            pltpu.VMEM((seq // tile, tile, tile), I16),
            pltpu.VMEM((seq // tile, tile, tile), F32),
            pltpu.VMEM((seq // tile, VT_ROWS, tile), BF16),
            pltpu.VMEM((IDX_HEADS, LANES, tile), BF16),
            pltpu.VMEM((HEADS, LANES, tile), BF16),
            pltpu.VMEM((HEADS, tile, tile), F32),
            pltpu.VMEM((HEADS, tile, tile), F32),
            pltpu.VMEM((HEADS, 1, tile), F32),
            pltpu.VMEM((HEADS, VT_ROWS, tile), F32),
        ],
        compiler_params=pltpu.CompilerParams(dimension_semantics=("parallel", "arbitrary"), vmem_limit_bytes=VMEM_LIMIT),
        name="dsa",
    )(z, z, z, z, zs, tril)


def _ret_kernel(q_ref, k_ref, v_ref, g_ref, dm_ref, xi_ref, zeta_ref, decay_ref, gain_ref, y_ref, r_ref):
    C = RET_CHUNK
    n_chunks = q_ref.shape[0] // C
    lo_half = _lane_lt_half_tile((C, LANES))
    r_ref[...] = jnp.zeros_like(r_ref)
    decay = decay_ref[0]
    block_diag = jnp.where(decay > 0.0, 1.0, 0.0)

    def head_stat(x):
        s_lo = jnp.sum(jnp.where(lo_half, x, 0.0), axis=-1, keepdims=True)
        s_hi = jnp.sum(jnp.where(lo_half, 0.0, x), axis=-1, keepdims=True)
        return jnp.where(lo_half, s_lo, s_hi) * (1.0 / HEAD_DIM)

    def body(c, _):
        rows = pl.ds(pl.multiple_of(c * C, C), C)
        q2, k2, v2 = q_ref[rows, :], k_ref[rows, :], v_ref[rows, :]
        zero = jnp.zeros_like(q2)
        inner = []
        for e in range(2):
            qe = jnp.where(lo_half, q2, zero) if e == 0 else jnp.where(lo_half, zero, q2)
            a = _dot_nt(qe, k2) * dm_ref[e]
            inner.append(_dot(a.astype(BF16), v2))
        state = r_ref[...]
        cross = _dot(q2, state.astype(BF16)) * xi_ref[0]
        r = jnp.where(lo_half, inner[0], inner[1]) + cross
        kz = (k2.astype(F32) * zeta_ref[0]).T.astype(BF16)
        r_ref[...] = state * decay + _dot(kz, v2) * block_diag
        mu = head_stat(r)
        d = r - mu
        var = head_stat(d * d)
        y = d * lax.rsqrt(var + EPS) * gain_ref[...]
        gate = g_ref[rows, :].astype(F32)
        y_ref[rows, :] = (y * (gate * jax.nn.sigmoid(gate))).astype(BF16)
        return 0

    lax.fori_loop(0, n_chunks, body, 0)


def _retention_tables():
    C = RET_CHUNK
    log_g = np.log(1.0 - 2.0 ** (-5.0 - np.arange(HEADS, dtype=np.float64)))
    n = np.arange(C, dtype=np.float64)
    diff = n[:, None] - n[None, :]
    dm = np.where(diff[None] >= 0, np.exp(diff[None] * log_g[:, None, None]), 0.0)
    xi = np.exp((n + 1.0)[None, :] * log_g[:, None])
    zeta = np.exp((C - 1.0 - n)[None, :] * log_g[:, None])
    g_c = np.exp(C * log_g)
    pair = lambda t: np.repeat(t.reshape(HEADS // 2, 2, C), HEAD_DIM, axis=1).transpose(0, 2, 1)
    decay = np.zeros((HEADS // 2, LANES, LANES))
    for p in range(HEADS // 2):
        decay[p, :HEAD_DIM, :HEAD_DIM] = g_c[2 * p]
        decay[p, HEAD_DIM:, HEAD_DIM:] = g_c[2 * p + 1]
    f = lambda t: jnp.asarray(t.astype(np.float32))
    return f(dm), f(pair(xi)), f(pair(zeta)), f(decay)


def _retention(z, gain_row, batch, seq):
    T = batch * seq
    C = RET_CHUNK
    dm, xi, zeta, decay = _retention_tables()
    pairs = HEADS // 2
    return pl.pallas_call(
        _ret_kernel,
        grid=(batch, pairs),
        in_specs=[
            pl.BlockSpec((seq, LANES), lambda b, p: (b, CB_QC + p)),
            pl.BlockSpec((seq, LANES), lambda b, p: (b, CB_KC + p)),
            pl.BlockSpec((seq, LANES), lambda b, p: (b, CB_VC + p)),
            pl.BlockSpec((seq, LANES), lambda b, p: (b, CB_GC + p)),
            pl.BlockSpec((2, C, C), lambda b, p: (p, 0, 0)),
            pl.BlockSpec((1, C, LANES), lambda b, p: (p, 0, 0)),
            pl.BlockSpec((1, C, LANES), lambda b, p: (p, 0, 0)),
            pl.BlockSpec((1, LANES, LANES), lambda b, p: (p, 0, 0)),
            pl.BlockSpec((1, LANES), lambda b, p: (0, p)),
        ],
        out_specs=pl.BlockSpec((seq, LANES), lambda b, p: (b, p)),
        out_shape=jax.ShapeDtypeStruct((T, BRANCH_WIDTH), BF16),
        scratch_shapes=[pltpu.VMEM((LANES, LANES), F32)],
        compiler_params=pltpu.CompilerParams(dimension_semantics=("parallel", "parallel")),
        name="retention",
    )(z, z, z, z, dm, xi, zeta, decay, gain_row)


def _swa_kernel(q_ref, kc_ref, kp_ref, vc_ref, vp_ref, sink_ref, y_ref):
    n = pl.program_id(1)
    W = SWA_WINDOW
    G = HEADS // SWA_KV_HEADS
    lo_half = _lane_lt_half_tile((W, LANES))
    qi = lax.broadcasted_iota(I32, (W, 2 * W), 0)
    kj = lax.broadcasted_iota(I32, (W, 2 * W), 1)
    band = (kj > qi) & (kj <= qi + W) & ((kj >= W) | (n > 0))
    for g in range(SWA_KV_HEADS):
        cols = slice(g * LANES, (g + 1) * LANES)
        kband = jnp.concatenate([kp_ref[:, cols], kc_ref[:, cols]], axis=0)
        vband = jnp.concatenate([vp_ref[:, cols], vc_ref[:, cols]], axis=0)
        q_rows, sink_rows = [], []
        for r in range(G):
            h = G * g + r
            t = q_ref[:, (h // 2) * LANES:(h // 2 + 1) * LANES]
            keep = lo_half if h % 2 == 0 else jnp.logical_not(lo_half)
            q_rows.append(jnp.where(keep, t, jnp.zeros_like(t)))
            sink_rows.append(jnp.broadcast_to(sink_ref[:, h:h + 1], (W, 1)))
        q4 = jnp.concatenate(q_rows, axis=0)
        sink = jnp.concatenate(sink_rows, axis=0)
        s = _dot_nt(q4, kband).reshape(G, W, 2 * W)
        s = jnp.where(band[None], s, MASKED).reshape(G * W, 2 * W)
        m = jnp.maximum(jnp.max(s, axis=-1, keepdims=True), sink)
        pexp = jnp.exp(s - m)
        denom = jnp.sum(pexp, axis=-1, keepdims=True) + jnp.exp(sink - m)
        o = _dot((pexp / denom).astype(BF16), vband)
        for r2 in range(G // 2):
            even = o[(2 * r2) * W:(2 * r2 + 1) * W, :]
            odd = o[(2 * r2 + 1) * W:(2 * r2 + 2) * W, :]
            pcol = (G * g) // 2 + r2
            y_ref[:, pcol * LANES:(pcol + 1) * LANES] = jnp.where(lo_half, even, odd).astype(BF16)


def _swa(z, sink_row, batch, seq):
    T = batch * seq
    W = SWA_WINDOW
    nb = seq // W
    cur = lambda cb: (lambda b, n: (b * nb + n, cb))
    prev = lambda cb: (lambda b, n: (b * nb + jnp.maximum(n - 1, 0), cb))
    return pl.pallas_call(
        _swa_kernel,
        grid=(batch, nb),
        in_specs=[
            pl.BlockSpec((W, BRANCH_WIDTH), cur(CB_QD // 4)),
            pl.BlockSpec((W, 2 * LANES), cur(CB_KD // 2)),
            pl.BlockSpec((W, 2 * LANES), prev(CB_KD // 2)),
            pl.BlockSpec((W, 2 * LANES), cur(CB_VD // 2)),
            pl.BlockSpec((W, 2 * LANES), prev(CB_VD // 2)),
            pl.BlockSpec((1, LANES), lambda b, n: (0, 0)),
        ],
        out_specs=pl.BlockSpec((W, BRANCH_WIDTH), lambda b, n: (b * nb + n, 0)),
        out_shape=jax.ShapeDtypeStruct((T, BRANCH_WIDTH), BF16),
        compiler_params=pltpu.CompilerParams(dimension_semantics=("parallel", "arbitrary")),
        name="swa",
    )(z, z, z, z, z, sink_row)


def _merge_kernel(x_ref, g_ref, ya_ref, yb_ref, yc_ref, yd_ref, wg_ref, wb_ref, wo_ref, o_ref):
    x = x_ref[...]
    D = x.shape[1]
    hb = _rmsnorm_f32(x, g_ref[...]).astype(BF16)
    merged = jnp.zeros(x.shape, F32)
    for n, y_ref in enumerate((ya_ref, yb_ref, yc_ref, yd_ref)):
        gate = jax.nn.sigmoid(_dot(hb, wg_ref[:, n * D:(n + 1) * D]))
        merged = merged + gate * _dot(y_ref[...], wb_ref[n])
    o_ref[...] = x + _dot(merged.astype(BF16), wo_ref[...])


def _merge(x2, g, ys, w_gates, w_branch, w_out):
    T, D = x2.shape
    tm = min(512, T)
    const = lambda shape: pl.BlockSpec(shape, lambda i: (0,) * len(shape), pipeline_mode=pl.Buffered(1))
    y_spec = pl.BlockSpec((tm, BRANCH_WIDTH), lambda i: (i, 0))
    return pl.pallas_call(
        _merge_kernel,
        grid=(T // tm,),
        in_specs=[
            pl.BlockSpec((tm, D), lambda i: (i, 0)),
            const((1, D)),
            y_spec, y_spec, y_spec, y_spec,
            const((D, N_BRANCH * D)),
            const((N_BRANCH, BRANCH_WIDTH, D)),
            const((D, D)),
        ],
        out_specs=pl.BlockSpec((tm, D), lambda i: (i, 0)),
        out_shape=jax.ShapeDtypeStruct((T, D), F32),
        compiler_params=pltpu.CompilerParams(dimension_semantics=("parallel",), vmem_limit_bytes=VMEM_LIMIT),
        name="merge",
    )(x2, g, *ys, w_gates, w_branch, w_out)


def _ffn_kernel(x_ref, g_ref, wup_ref, cw_ref, cb_ref, wdn_ref, fg_ref, o_ref, carry_ref, u_ref, *, tc, final_norm):
    n = pl.program_id(1)
    tm = x_ref.shape[0]
    d_ff = wdn_ref.shape[0]
    PAD = 8

    @pl.when(n == 0)
    def _():
        carry_ref[...] = jnp.zeros_like(carry_ref)

    x = x_ref[...]
    hb = _rmsnorm_f32(x, g_ref[...]).astype(BF16)

    def conv(col0):
        cols = slice(col0, col0 + tc)
        u = _dot(hb, wup_ref[:, cols])
        u_ref[0:PAD, :] = carry_ref[:, cols]
        u_ref[PAD:PAD + tm, :] = u
        carry_ref[:, cols] = u[tm - PAD:tm, :]
        w = cw_ref[:, cols]
        return (w[2:3, :] * u + w[1:2, :] * u_ref[PAD - 1:PAD - 1 + tm, :]
                + w[0:1, :] * u_ref[PAD - 2:PAD - 2 + tm, :] + cb_ref[:, cols])

    acc = jnp.zeros(x.shape, F32)
    for c in range(d_ff // tc):
        a = conv(c * tc)
        b = conv(d_ff + c * tc)
        act = (a * jax.nn.sigmoid(a) * b).astype(BF16)
        acc = acc + _dot(act, wdn_ref[c * tc:(c + 1) * tc, :])
    y = x + acc
    if final_norm:
        y = _rmsnorm_f32(y, fg_ref[...])
    o_ref[...] = y


def _ffn(x2, g, w_up, conv_w, conv_b, w_down, final_g, batch, seq, final_norm):
    T, D = x2.shape
    d_ff = w_down.shape[0]
    tm = min(512, seq)
    tc = 256
    nb = seq // tm
    const = lambda shape: pl.BlockSpec(shape, lambda b, n: (0,) * len(shape), pipeline_mode=pl.Buffered(1))
    return pl.pallas_call(
        functools.partial(_ffn_kernel, tc=tc, final_norm=final_norm),
        grid=(batch, nb),
        in_specs=[
            pl.BlockSpec((tm, D), lambda b, n: (b * nb + n, 0)),
            const((1, D)),
            const((D, 2 * d_ff)),
            const((CONV_WIDTH, 2 * d_ff)),
            const((1, 2 * d_ff)),
            const((d_ff, D)),
            const((1, D)),
        ],
        out_specs=pl.BlockSpec((tm, D), lambda b, n: (b * nb + n, 0)),
        out_shape=jax.ShapeDtypeStruct((T, D), F32),
        scratch_shapes=[pltpu.VMEM((8, 2 * d_ff), F32), pltpu.VMEM((tm + 8, tc), F32)],
        compiler_params=pltpu.CompilerParams(dimension_semantics=("parallel", "arbitrary"), vmem_limit_bytes=VMEM_LIMIT),
        name="ffn",
    )(x2, g, w_up, conv_w, conv_b, w_down, final_g)


def _rope_tables(seq):
    pos = jnp.arange(seq, dtype=F32)
    inv = 1.0 / (ROPE_THETA ** (jnp.arange(HALF, dtype=F32) * 2.0 / HEAD_DIM))
    ang = pos[:, None] * inv[None, :]
    cos, sin = jnp.cos(ang), jnp.sin(ang)
    cos_t = jnp.tile(cos, (1, LANES // HALF))
    sin_t = jnp.tile(jnp.concatenate([-sin, sin], axis=1), (1, LANES // HEAD_DIM))
    return cos_t, sin_t


def kernel(x, attn_norm, w_in, forget_bias, ret_norm, attn_sinks, w_branch, w_out, ffn_norm, w_up, conv_w, conv_b, w_down, final_norm):
    B, S, D = x.shape
    depth = w_in.shape[0]
    T = B * S
    cos_t, sin_t = _rope_tables(S)
    tile_fox = min(256, S)
    tile_dsa = min(256, S)
    pad_row = lambda v: jnp.pad(v.astype(F32), (0, LANES - v.shape[0]))[None, :]

    x2 = x.reshape(T, D)
    for l in range(depth):
        w_main, w_small, w_gates = _repack_w_in(w_in, l)

        z, zs = _inproj(x2, attn_norm[l][None, :], w_main, w_small, cos_t, sin_t, S)
        kf = _fcum(zs, pad_row(forget_bias[l]), z, S)

        y_a = _fox(z, kf, B, S, tile_fox)
        y_b = _dsa(z, zs, B, S, tile_dsa)
        y_c = _retention(z, ret_norm[l][None, :].astype(F32), B, S)
        y_d = _swa(z, pad_row(attn_sinks[l]), B, S)

        x2 = _merge(x2, attn_norm[l][None, :], (y_a, y_b, y_c, y_d), w_gates,
                    w_branch[l].astype(BF16), w_out[l].astype(BF16))
        x2 = _ffn(x2, ffn_norm[l][None, :], w_up[l].astype(BF16), conv_w[l], conv_b[l][None, :],
                  w_down[l].astype(BF16), final_norm[None, :], B, S, final_norm=(l == depth - 1))
    return x2.reshape(B, S, D)
```

```python
import functools

import numpy as np
import jax
import jax.numpy as jnp
from jax import lax
from jax.experimental import pallas as pl
from jax.experimental.pallas import tpu as pltpu

F32 = jnp.float32
BF16 = jnp.bfloat16
I32 = jnp.int32
I16 = jnp.int16
I16_MIN = -2 ** 15
I16_ROWS = 16

LANES = 128
HEAD_DIM = 64
HALF = HEAD_DIM // 2
HEADS = 8
BRANCH_WIDTH = HEADS * HEAD_DIM
IDX_HEADS = 4
SWA_KV_HEADS = 2
SWA_WINDOW = 128
RET_CHUNK = 128
RET_GROUP = 4
SWA_BLOCKS = 4
FFN_SLOTS = 4
N_BRANCH = 4
CONV_WIDTH = 3
DSA_TOPK_MAX = 256
ROPE_THETA = 10000.0
EPS = 1e-6
LOG2_E = 1.4426950408889634
VMEM_LIMIT = 56 * 1024 * 1024

VT_ROWS = HEAD_DIM + 16
F_PIECES = 3
M_INIT = -1e30
MASKED = -2e30
INT_MIN = -2 ** 31

CB_QA, CB_KA, CB_VA, CB_VC, CB_GC = 0, 4, 8, 12, 16
CB_VB, CB_VD = 20, 22
N_PLAIN_BLOCKS = 24
CB_QB, CB_QC, CB_KC, CB_QD, CB_QI = 24, 28, 32, 36, 40
CB_KB, CB_KI, CB_KD = 42, 43, 44
N_BLOCKS = 46
TN_IN = 256

_SIZES = (512, 512, 512, 8, 512, 64, 64, 256, 64, 4, 512, 512, 512, 512, 512, 128, 128, 4096)
_OFFS = np.concatenate([[0], np.cumsum(_SIZES)]).tolist()
(O_QA, O_KA, O_VA, O_FA, O_QB, O_KB, O_VB, O_QI, O_KI, O_WI,
 O_QC, O_KC, O_VC, O_GC, O_QD, O_KD, O_VD, O_GATES, N_IN) = _OFFS
ZS_WI = 8


def _column_plan():
    perm = np.full((N_BLOCKS * LANES,), N_IN, np.int32)
    scale = np.ones((N_BLOCKS * LANES,), np.float32)

    def put(block, off, width, s=1.0):
        perm[block * LANES: block * LANES + width] = np.arange(off, off + width)
        scale[block * LANES: block * LANES + width] = s

    q_scale = HEAD_DIM ** -0.5
    q_scale_log2 = q_scale * LOG2_E
    put(CB_QA, O_QA, 512, q_scale_log2); put(CB_KA, O_KA, 512); put(CB_VA, O_VA, 512)
    put(CB_VC, O_VC, 512); put(CB_GC, O_GC, 512)
    for half in range(2):
        perm[CB_VB * LANES + half * 64: CB_VB * LANES + half * 64 + 64] = np.arange(O_VB, O_VB + 64)
        perm[CB_KB * LANES + half * 64: CB_KB * LANES + half * 64 + 64] = np.arange(O_KB, O_KB + 64)
        perm[CB_KI * LANES + half * 64: CB_KI * LANES + half * 64 + 64] = np.arange(O_KI, O_KI + 64)
        for g in range(SWA_KV_HEADS):
            perm[(CB_VD + g) * LANES + half * 64: (CB_VD + g) * LANES + half * 64 + 64] = np.arange(O_VD + 64 * g, O_VD + 64 * g + 64)
            perm[(CB_KD + g) * LANES + half * 64: (CB_KD + g) * LANES + half * 64 + 64] = np.arange(O_KD + 64 * g, O_KD + 64 * g + 64)
    put(CB_QB, O_QB, 512, q_scale_log2); put(CB_QC, O_QC, 512); put(CB_KC, O_KC, 512, q_scale)
    put(CB_QD, O_QD, 512, q_scale); put(CB_QI, O_QI, 256)
    return perm, scale


def _segment_offsets(perm, scale):
    offs, n = [], 0
    for _, width in _column_segments(perm, scale):
        offs.append(n)
        n += width
    return offs


def _column_segments(perm, scale):
    segments, n = [], 0
    while n < len(perm):
        e = n + 1
        if perm[n] == N_IN:
            while e < len(perm) and perm[e] == N_IN:
                e += 1
            segments.append((None, e - n))
        else:
            while e < len(perm) and perm[e] == perm[e - 1] + 1 and scale[e] == scale[n]:
                e += 1
            segments.append((int(perm[n]), e - n))
        n = e
    return segments


def _rmsnorm_f32(x, g):
    ms = jnp.mean(x * x, axis=-1, keepdims=True)
    return x * lax.rsqrt(ms + EPS) * g


def _lane_lt_half_tile(shape):
    return lax.broadcasted_iota(I32, shape, len(shape) - 1) < HEAD_DIM


def _dot_nt(a, b):
    return lax.dot_general(a, b, (((1,), (1,)), ((), ())), preferred_element_type=F32)


def _dot(a, b):
    return jnp.dot(a, b, preferred_element_type=F32)


def _repack_kernel(w_ref, main_ref, small_ref, gates_ref, *, segments):
    rows = w_ref.shape[0]
    off = 0
    for start, width, s in segments:
        if start is None:
            main_ref[:, off:off + width] = jnp.zeros((rows, width), BF16)
        else:
            main_ref[:, off:off + width] = (w_ref[:, start:start + width] * s).astype(BF16)
        off += width
    small_ref[:, 0:HEADS] = w_ref[:, O_FA:O_FA + HEADS].astype(BF16)
    small_ref[:, ZS_WI:ZS_WI + IDX_HEADS] = w_ref[:, O_WI:O_WI + IDX_HEADS].astype(BF16)
    small_ref[:, ZS_WI + IDX_HEADS:] = jnp.zeros((rows, LANES - ZS_WI - IDX_HEADS), BF16)
    gates_ref[...] = w_ref[:, O_GATES:N_IN].astype(BF16)


def _repack_w_in(w_in, layer):
    _, D, n_in = w_in.shape
    perm, scale = _column_plan()
    segments = tuple((start, width, float(scale[off]))
                     for (start, width), off in zip(_column_segments(perm, scale), _segment_offsets(perm, scale)))
    tr = 128
    n_main = N_BLOCKS * LANES
    n_gates = N_IN - O_GATES
    return pl.pallas_call(
        functools.partial(_repack_kernel, segments=segments),
        grid=(D // tr,),
        in_specs=[pl.BlockSpec((None, tr, n_in), lambda r: (layer, r, 0))],
        out_specs=[pl.BlockSpec((tr, n_main), lambda r: (r, 0)), pl.BlockSpec((tr, LANES), lambda r: (r, 0)),
                   pl.BlockSpec((tr, n_gates), lambda r: (r, 0))],
        out_shape=[jax.ShapeDtypeStruct((D, n_main), BF16), jax.ShapeDtypeStruct((D, LANES), BF16),
                   jax.ShapeDtypeStruct((D, n_gates), BF16)],
        compiler_params=pltpu.CompilerParams(dimension_semantics=("parallel",), vmem_limit_bytes=VMEM_LIMIT),
        name="repack",
    )(w_in)


def _inproj_kernel(x_ref, g_ref, w_ref, ws_ref, cos_ref, sin_ref, z_ref, zs_ref, *, tn, n_plain_tiles):
    tm = x_ref.shape[0]
    hb = _rmsnorm_f32(x_ref[...], g_ref[...]).astype(BF16)
    zs_ref[...] = _dot(hb, ws_ref[...])
    lane = lax.broadcasted_iota(I32, (tm, LANES), 1)
    first_half = (lane % HEAD_DIM) < HALF
    for j in range(w_ref.shape[1] // tn):
        acc = _dot(hb, w_ref[:, j * tn:(j + 1) * tn])
        if j < n_plain_tiles:
            z_ref[:, j * tn:(j + 1) * tn] = acc.astype(BF16)
            continue
        for c in range(tn // LANES):
            a = acc[:, c * LANES:(c + 1) * LANES]
            rot = jnp.where(first_half, pltpu.roll(a, LANES - HALF, 1), pltpu.roll(a, HALF, 1))
            col = j * tn + c * LANES
            z_ref[:, col:col + LANES] = (a * cos_ref[...] + rot * sin_ref[...]).astype(BF16)


def _inproj(x2, g, w_main, w_small, cos_t, sin_t, seq):
    T, D = x2.shape
    N = w_main.shape[1]
    tm = min(512, seq)
    tn = TN_IN
    pos_blocks = seq // tm
    const = lambda shape: pl.BlockSpec(shape, lambda i: (0,) * len(shape), pipeline_mode=pl.Buffered(1))
    return pl.pallas_call(
        functools.partial(_inproj_kernel, tn=tn, n_plain_tiles=N_PLAIN_BLOCKS * LANES // tn),
        grid=(T // tm,),
        in_specs=[
            pl.BlockSpec((tm, D), lambda i: (i, 0)),
            const((1, D)),
            const((D, N)),
            const((D, LANES)),
            pl.BlockSpec((tm, LANES), lambda i: (i % pos_blocks, 0)),
            pl.BlockSpec((tm, LANES), lambda i: (i % pos_blocks, 0)),
        ],
        out_specs=[
            pl.BlockSpec((tm, N), lambda i: (i, 0)),
            pl.BlockSpec((tm, LANES), lambda i: (i, 0)),
        ],
        out_shape=[jax.ShapeDtypeStruct((T, N), BF16), jax.ShapeDtypeStruct((T, LANES), F32)],
        compiler_params=pltpu.CompilerParams(dimension_semantics=("parallel",), vmem_limit_bytes=VMEM_LIMIT),
        name="inproj",
    )(x2, g, w_main, w_small, cos_t, sin_t)


def _split3(x):
    hi = x.astype(BF16)
    r1 = x - hi.astype(F32)
    mid = r1.astype(BF16)
    lo = (r1 - mid.astype(F32)).astype(BF16)
    return hi, mid, lo


def _fcum_kernel(zs_ref, bias_ref, tri_ref, place_ref, k_ref, kf_ref):
    n_chunks = zs_ref.shape[0] // LANES
    lo_half = _lane_lt_half_tile((LANES, LANES))

    def body(c, carry):
        rows = pl.ds(pl.multiple_of(c * LANES, LANES), LANES)
        xv = zs_ref[rows, :] + bias_ref[...]
        logf = jnp.minimum(xv, 0.0) - jnp.log(1.0 + jnp.exp(-jnp.abs(xv)))
        tri = tri_ref[...]
        cs = carry
        for piece in _split3(logf):
            cs = cs + _dot(tri, piece)
        placed = jnp.zeros((LANES, HEADS * LANES), F32)
        for j, piece in enumerate(_split3(cs * LOG2_E)):
            placed = placed + _dot(piece, place_ref[j])
        for h in range(HEADS):
            kp = k_ref[rows, (h // 2) * LANES:(h // 2 + 1) * LANES]
            own = lo_half if h % 2 == 0 else jnp.logical_not(lo_half)
            kf_ref[rows, h * LANES:(h + 1) * LANES] = jnp.where(own, kp, placed[:, h * LANES:(h + 1) * LANES].astype(BF16))
        return cs[LANES - 1:LANES, :]

    lax.fori_loop(0, n_chunks, body, jnp.zeros((1, LANES), F32))


def _f_piece_row(h):
    return HEAD_DIM if h % 2 == 0 else 0


def _fcum(zs, bias_row, z, seq):
    T = zs.shape[0]
    tri = jnp.asarray(np.tril(np.ones((LANES, LANES), np.float32)), BF16)
    place = np.zeros((F_PIECES, LANES, HEADS * LANES), np.float32)
    for h in range(HEADS):
        for j in range(F_PIECES):
            place[j, h, h * LANES + _f_piece_row(h) + j] = 1.0
    return pl.pallas_call(
        _fcum_kernel,
        grid=(T // seq,),
        in_specs=[
            pl.BlockSpec((seq, LANES), lambda b: (b, 0)),
            pl.BlockSpec((1, LANES), lambda b: (0, 0)),
            pl.BlockSpec((LANES, LANES), lambda b: (0, 0)),
            pl.BlockSpec((F_PIECES, LANES, HEADS * LANES), lambda b: (0, 0, 0)),
            pl.BlockSpec((seq, BRANCH_WIDTH), lambda b: (b, CB_KA // 4)),
        ],
        out_specs=pl.BlockSpec((seq, HEADS * LANES), lambda b: (b, 0)),
        out_shape=jax.ShapeDtypeStruct((T, HEADS * LANES), BF16),
        compiler_params=pltpu.CompilerParams(dimension_semantics=("parallel",), vmem_limit_bytes=VMEM_LIMIT),
        name="fcum",
    )(zs, bias_row, tri, jnp.asarray(place, BF16), z)


def _flash_chunks(n_off, scores, values, sa_ref, sb_ref, m_ref, acc_ref):
    def produce(buf, kc, diagonal, h):
        buf[h] = scores(h, kc, diagonal)

    def consume(buf, kc, h):
        st = buf[h]
        m_old = m_ref[h]
        m_new = jnp.maximum(m_old, jnp.max(st, axis=0, keepdims=True))
        pt = jnp.exp2(st - m_new).astype(BF16)
        acc_ref[h] = jnp.exp2(m_old - m_new) * acc_ref[h] + _dot(values(h, kc), pt)
        m_ref[h] = m_new

    def consume_and_produce(cbuf, ckc, pbuf, pkc):
        for h in range(HEADS):
            produce(pbuf, pkc, False, h)
            consume(cbuf, ckc, h)

    def chunk_of_item(t):
        return jnp.where(t == 0, n_off, t - 1)

    for h in range(HEADS):
        produce(sa_ref, n_off, True, h)

    def pair(j, carry):
        consume_and_produce(sa_ref, chunk_of_item(2 * j), sb_ref, 2 * j)
        consume_and_produce(sb_ref, 2 * j, sa_ref, 2 * j + 1)
        return carry

    lax.fori_loop(0, n_off // 2, pair, 0)

    @pl.when(n_off % 2 == 0)
    def _():
        for h in range(HEADS):
            consume(sa_ref, chunk_of_item(n_off), h)

    @pl.when(n_off % 2 == 1)
    def _():
        consume_and_produce(sa_ref, chunk_of_item(n_off - 1), sb_ref, n_off - 1)
        for h in range(HEADS):
            consume(sb_ref, n_off - 1, h)


def _fox_kernel(q_ref, k_ref, v_ref, y_ref, vt_ref, qt_ref, sa_ref, sb_ref, m_ref, acc_ref, *, tq, tk):
    i = pl.program_id(1)
    n_kc = k_ref.shape[0] // tk
    pairs = HEADS // 2

    @pl.when(i == 0)
    def _():
        ones_row = jnp.where(lax.broadcasted_iota(I32, (VT_ROWS - HEAD_DIM, tk), 0) == 0, 1.0, 0.0).astype(BF16)
        for p in range(pairs):
            for c in range(n_kc):
                t = v_ref[c * tk:(c + 1) * tk, p * LANES:(p + 1) * LANES].astype(F32).T.astype(BF16)
                for e in range(2):
                    vt_ref[2 * p + e, c, 0:HEAD_DIM, :] = t[HEAD_DIM * e:HEAD_DIM * (e + 1), :]
                    vt_ref[2 * p + e, c, HEAD_DIM:VT_ROWS, :] = ones_row

    row = lax.broadcasted_iota(I32, (LANES, tq), 0)
    for p in range(pairs):
        qt = q_ref[:, p * LANES:(p + 1) * LANES].astype(F32).T
        for e in range(2):
            own = (row < HEAD_DIM) if e == 0 else (row >= HEAD_DIM)
            f0 = _f_piece_row(e)
            minus_one = jnp.where((row >= f0) & (row < f0 + F_PIECES), -1.0, 0.0)
            qt_ref[2 * p + e] = jnp.where(own, qt, minus_one).astype(BF16)
    m_ref[...] = jnp.full(m_ref.shape, M_INIT, F32)
    acc_ref[...] = jnp.zeros(acc_ref.shape, F32)

    def scores(h, kc, diagonal):
        k0 = pl.multiple_of(kc * tk, tk)
        st = _dot(k_ref[pl.ds(k0, tk), h * LANES:(h + 1) * LANES], qt_ref[h])
        if diagonal:
            key_pos = k0 + lax.broadcasted_iota(I32, (tk, 1), 0)
            query_pos = i * tq + lax.broadcasted_iota(I32, (1, tq), 1)
            st = jnp.where(key_pos <= query_pos, st, MASKED)
        return st

    _flash_chunks(i, scores, lambda h, kc: vt_ref[h, kc], sa_ref, sb_ref, m_ref, acc_ref)

    for p in range(pairs):
        heads = []
        for h in (2 * p, 2 * p + 1):
            heads.append(acc_ref[h, 0:HEAD_DIM, :] / acc_ref[h, HEAD_DIM:HEAD_DIM + 1, :])
        y_ref[:, p * LANES:(p + 1) * LANES] = jnp.concatenate(heads, axis=0).T.astype(BF16)


def _fox(z, kf, batch, seq, tile):
    nq = seq // tile
    T = batch * seq
    return pl.pallas_call(
        functools.partial(_fox_kernel, tq=tile, tk=tile),
        grid=(batch, nq),
        in_specs=[
            pl.BlockSpec((tile, BRANCH_WIDTH), lambda b, i: (b * nq + i, CB_QA // 4)),
            pl.BlockSpec((seq, HEADS * LANES), lambda b, i: (b, 0)),
            pl.BlockSpec((seq, BRANCH_WIDTH), lambda b, i: (b, CB_VA // 4)),
        ],
        out_specs=pl.BlockSpec((tile, BRANCH_WIDTH), lambda b, i: (b * nq + i, 0)),
        out_shape=jax.ShapeDtypeStruct((T, BRANCH_WIDTH), BF16),
        scratch_shapes=[
            pltpu.VMEM((HEADS, seq // tile, VT_ROWS, tile), BF16),
            pltpu.VMEM((HEADS, LANES, tile), BF16),
            pltpu.VMEM((HEADS, tile, tile), F32),
            pltpu.VMEM((HEADS, tile, tile), F32),
            pltpu.VMEM((HEADS, 1, tile), F32),
            pltpu.VMEM((HEADS, VT_ROWS, tile), F32),
        ],
        compiler_params=pltpu.CompilerParams(dimension_semantics=("parallel", "arbitrary"), vmem_limit_bytes=VMEM_LIMIT),
        name="fox",
    )(z, kf, z)


def _half_rows_operand(qt, first_half):
    row = lax.broadcasted_iota(I32, qt.shape, 0)
    own = (row < HEAD_DIM) if first_half else (row >= HEAD_DIM)
    return jnp.where(own, qt, 0.0).astype(BF16)


def _count16(ref, n_chunks, pred):
    ck, tq = ref.shape[1:]

    def body(kc, acc):
        hit = jnp.where(pred(ref[kc]), jnp.int16(1), jnp.int16(0))
        for r in range(ck // I16_ROWS):
            acc = acc + hit[r * I16_ROWS:(r + 1) * I16_ROWS, :]
        return acc

    acc = lax.fori_loop(0, n_chunks, body, jnp.zeros((I16_ROWS, tq), I16))
    return jnp.sum(acc.astype(I32), axis=0, keepdims=True)


def _search16(ref, n_chunks, want):
    v0 = jnp.where(_count16(ref, n_chunks, lambda k: k >= 0) >= want, 0, I16_MIN).astype(I32)

    def bit_body(b, v):
        cand = v | jnp.left_shift(jnp.int32(1), 14 - b)
        c = _count16(ref, n_chunks, lambda k: k >= cand.astype(I16))
        return jnp.where(c >= want, cand, v)

    return lax.fori_loop(0, 15, bit_body, v0)


def _topk_threshold(key_ref, hi_ref, lo_ref, n_chunks, topk):
    hi_v = _search16(hi_ref, n_chunks, topk)
    want_lo = topk - _count16(hi_ref, n_chunks, lambda k: k > hi_v.astype(I16))

    def fill_lo(kc, carry):
        key = key_ref[kc]
        lo = (key & 0xFFFF) - 32768
        lo_ref[kc] = jnp.where((key >> 16) == hi_v, lo, I16_MIN).astype(I16)
        return carry

    lax.fori_loop(0, n_chunks, fill_lo, 0)
    lo_v = _search16(lo_ref, n_chunks, want_lo)
    need = want_lo - _count16(lo_ref, n_chunks, lambda k: k > lo_v.astype(I16))
    v = hi_v * 65536 + (lo_v + 32768)
    return v, need.astype(F32)


def _dsa_kernel(qi_ref, kk_ref, vv_ref, qb_ref, zs_ref, tril_ref, y_ref,
                key_ref, hi_ref, lo_ref, bias_ref, vt_ref, qit_ref, qbt_ref, sa_ref, sb_ref, m_ref, acc_ref, *, tq, ck, topk):
    i = pl.program_id(1)
    n_chunks = i + 1
    n_kc_total = kk_ref.shape[0] // ck

    @pl.when(i == 0)
    def _():
        ones_row = jnp.where(lax.broadcasted_iota(I32, (VT_ROWS - HEAD_DIM, ck), 0) == 0, 1.0, 0.0).astype(BF16)
        for c in range(n_kc_total):
            t = vv_ref[c * ck:(c + 1) * ck, :].astype(F32).T.astype(BF16)
            vt_ref[c, 0:HEAD_DIM, :] = t[0:HEAD_DIM, :]
            vt_ref[c, HEAD_DIM:VT_ROWS, :] = ones_row

    for p in range(IDX_HEADS // 2):
        qt = qi_ref[:, p * LANES:(p + 1) * LANES].astype(F32).T
        qit_ref[2 * p] = _half_rows_operand(qt, True)
        qit_ref[2 * p + 1] = _half_rows_operand(qt, False)
    for p in range(HEADS // 2):
        qt = qb_ref[:, p * LANES:(p + 1) * LANES].astype(F32).T
        qbt_ref[2 * p] = _half_rows_operand(qt, True)
        qbt_ref[2 * p + 1] = _half_rows_operand(qt, False)
    w_scale = (IDX_HEADS ** -0.5) * (HEAD_DIM ** -0.5)
    w_rows = zs_ref[...].T[ZS_WI:ZS_WI + IDX_HEADS, :] * w_scale

    def score_chunk(kc, diagonal):
        k0 = pl.multiple_of(kc * ck, ck)
        ki = kk_ref[pl.ds(k0, ck), LANES:2 * LANES]
        s = jnp.zeros((ck, tq), F32)
        for h in range(IDX_HEADS):
            s = s + jnp.maximum(_dot(ki, qit_ref[h]), 0.0) * w_rows[h:h + 1, :]
        s = jnp.where(s == 0.0, 0.0, s)
        bits = pltpu.bitcast(s, I32)
        key = bits ^ ((bits >> 31) & 0x7FFFFFFF)
        if diagonal:
            key_pos = k0 + lax.broadcasted_iota(I32, (ck, 1), 0)
            query_pos = i * tq + lax.broadcasted_iota(I32, (1, tq), 1)
            key = jnp.where(key_pos <= query_pos, key, INT_MIN)
        key_ref[kc] = key
        hi_ref[kc] = (key >> 16).astype(I16)

    def score_body(kc, carry):
        score_chunk(kc, False)
        return carry

    lax.fori_loop(0, i, score_body, 0)
    score_chunk(i, True)

    v, need = _topk_threshold(key_ref, hi_ref, lo_ref, n_chunks, topk)

    def bias_body(kc, carry):
        key = key_ref[kc]
        eq = (key == v) & (key > INT_MIN)
        eqf = jnp.where(eq, 1.0, 0.0)
        prefix = _dot(tril_ref[...], eqf.astype(BF16)) + carry
        sel = (key > v) | (eq & (prefix <= need))
        bias_ref[kc] = jnp.where(sel, 0.0, MASKED)
        return carry + jnp.sum(eqf, axis=0, keepdims=True)

    lax.fori_loop(0, n_chunks, bias_body, jnp.zeros((1, tq), F32))

    m_ref[...] = jnp.full(m_ref.shape, M_INIT, F32)
    acc_ref[...] = jnp.zeros(acc_ref.shape, F32)

    def scores(h, kc, diagonal):
        k0 = pl.multiple_of(kc * ck, ck)
        return _dot(kk_ref[pl.ds(k0, ck), 0:LANES], qbt_ref[h]) + bias_ref[kc]

    _flash_chunks(i, scores, lambda h, kc: vt_ref[kc], sa_ref, sb_ref, m_ref, acc_ref)

    for p in range(HEADS // 2):
        heads = []
        for h in (2 * p, 2 * p + 1):
            heads.append(acc_ref[h, 0:HEAD_DIM, :] / acc_ref[h, HEAD_DIM:HEAD_DIM + 1, :])
        y_ref[:, p * LANES:(p + 1) * LANES] = jnp.concatenate(heads, axis=0).T.astype(BF16)


def _dsa(z, zs, batch, seq, tile):
    nq = seq // tile
    T = batch * seq
    topk = min(DSA_TOPK_MAX, seq // 4)
    tril = jnp.asarray(np.tril(np.ones((tile, tile), np.float32)), BF16)
    return pl.pallas_call(
        functools.partial(_dsa_kernel, tq=tile, ck=tile, topk=topk),
        grid=(batch, nq),
        in_specs=[
            pl.BlockSpec((tile, 2 * LANES), lambda b, i: (b * nq + i, CB_QI // 2)),
            pl.BlockSpec((seq, 2 * LANES), lambda b, i: (b, CB_KB // 2)),
            pl.BlockSpec((seq, LANES), lambda b, i: (b, CB_VB)),
            pl.BlockSpec((tile, BRANCH_WIDTH), lambda b, i: (b * nq + i, CB_QB // 4)),
            pl.BlockSpec((tile, LANES), lambda b, i: (b * nq + i, 0)),
            pl.BlockSpec((tile, tile), lambda b, i: (0, 0)),
        ],
        out_specs=pl.BlockSpec((tile, BRANCH_WIDTH), lambda b, i: (b * nq + i, 0)),
        out_shape=jax.ShapeDtypeStruct((T, BRANCH_WIDTH), BF16),
        scratch_shapes=[
            pltpu.VMEM((seq // tile, tile, tile), I32),
            pltpu.VMEM((seq // tile, tile, tile), I16),
            pltpu.VMEM((seq // tile, tile, tile), I16),
            pltpu.VMEM((seq // tile, tile, tile), F32),
            pltpu.VMEM((seq // tile, VT_ROWS, tile), BF16),
            pltpu.VMEM((IDX_HEADS, LANES, tile), BF16),
            pltpu.VMEM((HEADS, LANES, tile), BF16),
            pltpu.VMEM((HEADS, tile, tile), F32),
            pltpu.VMEM((HEADS, tile, tile), F32),
            pltpu.VMEM((HEADS, 1, tile), F32),
            pltpu.VMEM((HEADS, VT_ROWS, tile), F32),
        ],
        compiler_params=pltpu.CompilerParams(dimension_semantics=("parallel", "arbitrary"), vmem_limit_bytes=VMEM_LIMIT),
        name="dsa",
    )(z, z, z, z, zs, tril)


def _ret_kernel(q_ref, k_ref, v_ref, g_ref, dm_ref, xi_ref, zeta_ref, decay_ref, gain_ref, y_ref):
    C = RET_CHUNK
    n_chunks = q_ref.shape[0] // C
    lo_half = _lane_lt_half_tile((C, LANES))
    decay = decay_ref[0]
    block_diag = jnp.where(decay > 0.0, 1.0, 0.0)

    def head_stat(x):
        s_lo = jnp.sum(jnp.where(lo_half, x, 0.0), axis=-1, keepdims=True)
        s_hi = jnp.sum(jnp.where(lo_half, 0.0, x), axis=-1, keepdims=True)
        return jnp.where(lo_half, s_lo, s_hi) * (1.0 / HEAD_DIM)

    def group(gi, state):
        rows = [pl.ds(pl.multiple_of((gi * RET_GROUP + c) * C, C), C) for c in range(RET_GROUP)]
        q = [q_ref[r, :] for r in rows]
        k = [k_ref[r, :] for r in rows]
        v = [v_ref[r, :] for r in rows]
        qk, upd = [], []
        for c in range(RET_GROUP):
            zero = jnp.zeros_like(q[c])
            heads = (jnp.where(lo_half, q[c], zero), jnp.where(lo_half, zero, q[c]))
            qk.append([_dot_nt(heads[e], k[c]) for e in range(2)])
            kz = (k[c].astype(F32) * zeta_ref[0]).T.astype(BF16)
            upd.append(_dot(kz, v[c]) * block_diag)
        cross = []
        for c in range(RET_GROUP):
            cross.append(_dot(q[c], state.astype(BF16)) * xi_ref[0])
            state = state * decay + upd[c]
        for c in range(RET_GROUP):
            inner = [_dot((qk[c][e] * dm_ref[e]).astype(BF16), v[c]) for e in range(2)]
            r = jnp.where(lo_half, inner[0], inner[1]) + cross[c]
            mu = head_stat(r)
            d = r - mu
            var = head_stat(d * d)
            y = d * lax.rsqrt(var + EPS) * gain_ref[...]
            gate = g_ref[rows[c], :].astype(F32)
            y_ref[rows[c], :] = (y * (gate * jax.nn.sigmoid(gate))).astype(BF16)
        return state

    lax.fori_loop(0, n_chunks // RET_GROUP, group, jnp.zeros((LANES, LANES), F32))


def _retention_tables():
    C = RET_CHUNK
    log_g = np.log(1.0 - 2.0 ** (-5.0 - np.arange(HEADS, dtype=np.float64)))
    n = np.arange(C, dtype=np.float64)
    diff = n[:, None] - n[None, :]
    dm = np.where(diff[None] >= 0, np.exp(diff[None] * log_g[:, None, None]), 0.0)
    xi = np.exp((n + 1.0)[None, :] * log_g[:, None])
    zeta = np.exp((C - 1.0 - n)[None, :] * log_g[:, None])
    g_c = np.exp(C * log_g)
    pair = lambda t: np.repeat(t.reshape(HEADS // 2, 2, C), HEAD_DIM, axis=1).transpose(0, 2, 1)
    decay = np.zeros((HEADS // 2, LANES, LANES))
    for p in range(HEADS // 2):
        decay[p, :HEAD_DIM, :HEAD_DIM] = g_c[2 * p]
        decay[p, HEAD_DIM:, HEAD_DIM:] = g_c[2 * p + 1]
    f = lambda t: jnp.asarray(t.astype(np.float32))
    return f(dm), f(pair(xi)), f(pair(zeta)), f(decay)


def _retention(z, gain_row, batch, seq):
    T = batch * seq
    C = RET_CHUNK
    dm, xi, zeta, decay = _retention_tables()
    pairs = HEADS // 2
    return pl.pallas_call(
        _ret_kernel,
        grid=(batch, pairs),
        in_specs=[
            pl.BlockSpec((seq, LANES), lambda b, p: (b, CB_QC + p)),
            pl.BlockSpec((seq, LANES), lambda b, p: (b, CB_KC + p)),
            pl.BlockSpec((seq, LANES), lambda b, p: (b, CB_VC + p)),
            pl.BlockSpec((seq, LANES), lambda b, p: (b, CB_GC + p)),
            pl.BlockSpec((2, C, C), lambda b, p: (p, 0, 0)),
            pl.BlockSpec((1, C, LANES), lambda b, p: (p, 0, 0)),
            pl.BlockSpec((1, C, LANES), lambda b, p: (p, 0, 0)),
            pl.BlockSpec((1, LANES, LANES), lambda b, p: (p, 0, 0)),
            pl.BlockSpec((1, LANES), lambda b, p: (0, p)),
        ],
        out_specs=pl.BlockSpec((seq, LANES), lambda b, p: (b, p)),
        out_shape=jax.ShapeDtypeStruct((T, BRANCH_WIDTH), BF16),
        compiler_params=pltpu.CompilerParams(dimension_semantics=("parallel", "parallel")),
        name="retention",
    )(z, z, z, z, dm, xi, zeta, decay, gain_row)


def _swa_kernel(q_ref, kc_ref, kp_ref, vc_ref, vp_ref, sink_ref, y_ref, *, blocks):
    n = pl.program_id(1)
    W = SWA_WINDOW
    G = HEADS // SWA_KV_HEADS
    lo_half = _lane_lt_half_tile((W, LANES))
    qi = lax.broadcasted_iota(I32, (W, 2 * W), 0)
    kj = lax.broadcasted_iota(I32, (W, 2 * W), 1)
    in_band = (kj > qi) & (kj <= qi + W)
    chains = [(j, g) for j in range(blocks) for g in range(SWA_KV_HEADS)]

    def band_of(ref_prev, ref_cur, j, cols):
        if j == 0:
            return jnp.concatenate([ref_prev[:, cols], ref_cur[0:W, cols]], axis=0)
        return ref_cur[(j - 1) * W:(j + 1) * W, cols]

    sinks = {}
    for g in range(SWA_KV_HEADS):
        sinks[g] = jnp.concatenate(
            [jnp.broadcast_to(sink_ref[:, G * g + r:G * g + r + 1], (W, 1)) for r in range(G)], axis=0)

    scores = {}
    for j, g in chains:
        cols = slice(g * LANES, (g + 1) * LANES)
        q_rows = []
        for r in range(G):
            h = G * g + r
            t = q_ref[j * W:(j + 1) * W, (h // 2) * LANES:(h // 2 + 1) * LANES]
            keep = lo_half if h % 2 == 0 else jnp.logical_not(lo_half)
            q_rows.append(jnp.where(keep, t, jnp.zeros_like(t)))
        s = _dot_nt(jnp.concatenate(q_rows, axis=0), band_of(kp_ref, kc_ref, j, cols)).reshape(G, W, 2 * W)
        band = in_band & ((kj >= W) | (n > 0)) if j == 0 else in_band
        scores[j, g] = jnp.where(band[None], s, MASKED).reshape(G * W, 2 * W)

    probs, sink_terms = {}, {}
    for j, g in chains:
        s = scores[j, g]
        m = jnp.maximum(jnp.max(s, axis=-1, keepdims=True), sinks[g])
        probs[j, g] = jnp.exp(s - m).astype(BF16)
        sink_terms[j, g] = jnp.exp(sinks[g] - m)

    ones = jnp.ones((2 * W, LANES), BF16)
    for j, g in chains:
        cols = slice(g * LANES, (g + 1) * LANES)
        denom = _dot(probs[j, g], ones) + sink_terms[j, g]
        o = _dot(probs[j, g], band_of(vp_ref, vc_ref, j, cols)) / denom
        for r2 in range(G // 2):
            even = o[(2 * r2) * W:(2 * r2 + 1) * W, :]
            odd = o[(2 * r2 + 1) * W:(2 * r2 + 2) * W, :]
            pcol = (G * g) // 2 + r2
            y_ref[j * W:(j + 1) * W, pcol * LANES:(pcol + 1) * LANES] = jnp.where(lo_half, even, odd).astype(BF16)


def _swa(z, sink_row, batch, seq):
    T = batch * seq
    W = SWA_WINDOW
    blocks = min(SWA_BLOCKS, seq // W)
    rows = blocks * W
    ns = seq // rows
    cur = lambda cb: (lambda b, n: (b * ns + n, cb))
    prev = lambda cb: (lambda b, n: (jnp.maximum((b * ns + n) * blocks - 1, b * ns * blocks), cb))
    return pl.pallas_call(
        functools.partial(_swa_kernel, blocks=blocks),
        grid=(batch, ns),
        in_specs=[
            pl.BlockSpec((rows, BRANCH_WIDTH), cur(CB_QD // 4)),
            pl.BlockSpec((rows, 2 * LANES), cur(CB_KD // 2)),
            pl.BlockSpec((W, 2 * LANES), prev(CB_KD // 2)),
            pl.BlockSpec((rows, 2 * LANES), cur(CB_VD // 2)),
            pl.BlockSpec((W, 2 * LANES), prev(CB_VD // 2)),
            pl.BlockSpec((1, LANES), lambda b, n: (0, 0)),
        ],
        out_specs=pl.BlockSpec((rows, BRANCH_WIDTH), lambda b, n: (b * ns + n, 0)),
        out_shape=jax.ShapeDtypeStruct((T, BRANCH_WIDTH), BF16),
        compiler_params=pltpu.CompilerParams(dimension_semantics=("parallel", "arbitrary"), vmem_limit_bytes=VMEM_LIMIT),
        name="swa",
    )(z, z, z, z, z, sink_row)


def _merge_kernel(x_ref, g_ref, ya_ref, yb_ref, yc_ref, yd_ref, wg_ref, wb_ref, wo_ref, o_ref):
    x = x_ref[...]
    D = x.shape[1]
    hb = _rmsnorm_f32(x, g_ref[...]).astype(BF16)
    merged = jnp.zeros(x.shape, F32)
    for n, y_ref in enumerate((ya_ref, yb_ref, yc_ref, yd_ref)):
        gate = jax.nn.sigmoid(_dot(hb, wg_ref[:, n * D:(n + 1) * D]))
        merged = merged + gate * _dot(y_ref[...], wb_ref[n])
    o_ref[...] = x + _dot(merged.astype(BF16), wo_ref[...])


def _merge(x2, g, ys, w_gates, w_branch, w_out):
    T, D = x2.shape
    tm = min(512, T)
    const = lambda shape: pl.BlockSpec(shape, lambda i: (0,) * len(shape), pipeline_mode=pl.Buffered(1))
    y_spec = pl.BlockSpec((tm, BRANCH_WIDTH), lambda i: (i, 0))
    return pl.pallas_call(
        _merge_kernel,
        grid=(T // tm,),
        in_specs=[
            pl.BlockSpec((tm, D), lambda i: (i, 0)),
            const((1, D)),
            y_spec, y_spec, y_spec, y_spec,
            const((D, N_BRANCH * D)),
            const((N_BRANCH, BRANCH_WIDTH, D)),
            const((D, D)),
        ],
        out_specs=pl.BlockSpec((tm, D), lambda i: (i, 0)),
        out_shape=jax.ShapeDtypeStruct((T, D), F32),
        compiler_params=pltpu.CompilerParams(dimension_semantics=("parallel",), vmem_limit_bytes=VMEM_LIMIT),
        name="merge",
    )(x2, g, *ys, w_gates, w_branch, w_out)


def _ffn_kernel(x_ref, g_ref, wup_ref, cw_ref, cb_ref, wdn_ref, fg_ref, o_ref, carry_ref, u_ref, act_ref,
                *, tc, final_norm):
    n = pl.program_id(1)
    tm = x_ref.shape[0]
    d_ff = wdn_ref.shape[0]
    PAD = 8
    n_slots = u_ref.shape[0]

    @pl.when(n == 0)
    def _():
        carry_ref[...] = jnp.zeros_like(carry_ref)

    x = x_ref[...]
    hb = _rmsnorm_f32(x, g_ref[...]).astype(BF16)

    def conv(col0, slot):
        cols = slice(col0, col0 + tc)
        u = _dot(hb, wup_ref[:, cols])
        u_ref[slot, 0:PAD, :] = carry_ref[:, cols]
        u_ref[slot, PAD:PAD + tm, :] = u
        carry_ref[:, cols] = u[tm - PAD:tm, :]
        w = cw_ref[:, cols]
        return (w[2:3, :] * u + w[1:2, :] * u_ref[slot, PAD - 1:PAD - 1 + tm, :]
                + w[0:1, :] * u_ref[slot, PAD - 2:PAD - 2 + tm, :] + cb_ref[:, cols])

    for c in range(d_ff // tc):
        a = conv(c * tc, (2 * c) % n_slots)
        b = conv(d_ff + c * tc, (2 * c + 1) % n_slots)
        act_ref[:, c * tc:(c + 1) * tc] = (a * jax.nn.sigmoid(a) * b).astype(BF16)
    y = x + _dot(act_ref[...], wdn_ref[...])
    if final_norm:
        y = _rmsnorm_f32(y, fg_ref[...])
    o_ref[...] = y


def _ffn(x2, g, w_up, conv_w, conv_b, w_down, final_g, batch, seq, final_norm):
    T, D = x2.shape
    d_ff = w_down.shape[0]
    tm = min(512, seq)
    tc = 256
    nb = seq // tm
    const = lambda shape: pl.BlockSpec(shape, lambda b, n: (0,) * len(shape), pipeline_mode=pl.Buffered(1))
    return pl.pallas_call(
        functools.partial(_ffn_kernel, tc=tc, final_norm=final_norm),
        grid=(batch, nb),
        in_specs=[
            pl.BlockSpec((tm, D), lambda b, n: (b * nb + n, 0)),
            const((1, D)),
            const((D, 2 * d_ff)),
            const((CONV_WIDTH, 2 * d_ff)),
            const((1, 2 * d_ff)),
            const((d_ff, D)),
            const((1, D)),
        ],
        out_specs=pl.BlockSpec((tm, D), lambda b, n: (b * nb + n, 0)),
        out_shape=jax.ShapeDtypeStruct((T, D), F32),
        scratch_shapes=[pltpu.VMEM((8, 2 * d_ff), F32), pltpu.VMEM((FFN_SLOTS, tm + 8, tc), F32),
                        pltpu.VMEM((tm, d_ff), BF16)],
        compiler_params=pltpu.CompilerParams(dimension_semantics=("parallel", "arbitrary"), vmem_limit_bytes=VMEM_LIMIT),
        name="ffn",
    )(x2, g, w_up, conv_w, conv_b, w_down, final_g)


def _rope_tables(seq):
    pos = jnp.arange(seq, dtype=F32)
    inv = 1.0 / (ROPE_THETA ** (jnp.arange(HALF, dtype=F32) * 2.0 / HEAD_DIM))
    ang = pos[:, None] * inv[None, :]
    cos, sin = jnp.cos(ang), jnp.sin(ang)
    cos_t = jnp.tile(cos, (1, LANES // HALF))
    sin_t = jnp.tile(jnp.concatenate([-sin, sin], axis=1), (1, LANES // HEAD_DIM))
    return cos_t, sin_t


def kernel(x, attn_norm, w_in, forget_bias, ret_norm, attn_sinks, w_branch, w_out, ffn_norm, w_up, conv_w, conv_b, w_down, final_norm):
    B, S, D = x.shape
    depth = w_in.shape[0]
    T = B * S
    cos_t, sin_t = _rope_tables(S)
    tile_fox = min(256, S)
    tile_dsa = min(256, S)
    pad_row = lambda v: jnp.pad(v.astype(F32), (0, LANES - v.shape[0]))[None, :]

    x2 = x.reshape(T, D)
    for l in range(depth):
        w_main, w_small, w_gates = _repack_w_in(w_in, l)

        z, zs = _inproj(x2, attn_norm[l][None, :], w_main, w_small, cos_t, sin_t, S)
        kf = _fcum(zs, pad_row(forget_bias[l]), z, S)

        y_a = _fox(z, kf, B, S, tile_fox)
        y_b = _dsa(z, zs, B, S, tile_dsa)
        y_c = _retention(z, ret_norm[l][None, :].astype(F32), B, S)
        y_d = _swa(z, pad_row(attn_sinks[l]), B, S)

        x2 = _merge(x2, attn_norm[l][None, :], (y_a, y_b, y_c, y_d), w_gates,
                    w_branch[l].astype(BF16), w_out[l].astype(BF16))
        x2 = _ffn(x2, ffn_norm[l][None, :], w_up[l].astype(BF16), conv_w[l], conv_b[l][None, :],
                  w_down[l].astype(BF16), final_norm[None, :], B, S, final_norm=(l == depth - 1))
    return x2.reshape(B, S, D)
```

```python
import functools

import numpy as np
import jax
import jax.numpy as jnp
from jax import lax
from jax.experimental import pallas as pl
from jax.experimental.pallas import tpu as pltpu

F32 = jnp.float32
BF16 = jnp.bfloat16
I32 = jnp.int32
I16 = jnp.int16
I16_MIN = -2 ** 15
I16_ROWS = 16

LANES = 128
HEAD_DIM = 64
HALF = HEAD_DIM // 2
HEADS = 8
BRANCH_WIDTH = HEADS * HEAD_DIM
IDX_HEADS = 4
SWA_KV_HEADS = 2
SWA_WINDOW = 128
RET_CHUNK = 128
RET_GROUP = 4
SWA_BLOCKS = 4
FFN_SLOTS = 4
FCUM_UNROLL = 4
N_BRANCH = 4
CONV_WIDTH = 3
DSA_TOPK_MAX = 256
ROPE_THETA = 10000.0
EPS = 1e-6
LOG2_E = 1.4426950408889634
VMEM_LIMIT = 56 * 1024 * 1024

VT_ROWS = HEAD_DIM + 16
F_PIECES = 3
M_INIT = -1e30
MASKED = -2e30
INT_MIN = -2 ** 31

CB_QA, CB_KA, CB_VA, CB_VC, CB_GC = 0, 4, 8, 12, 16
CB_VB, CB_VD = 20, 22
N_PLAIN_BLOCKS = 24
CB_QB, CB_QC, CB_KC, CB_QD, CB_QI = 24, 28, 32, 36, 40
CB_KB, CB_KI, CB_KD = 42, 43, 44
N_BLOCKS = 46
TN_IN = 256

_SIZES = (512, 512, 512, 8, 512, 64, 64, 256, 64, 4, 512, 512, 512, 512, 512, 128, 128, 4096)
_OFFS = np.concatenate([[0], np.cumsum(_SIZES)]).tolist()
(O_QA, O_KA, O_VA, O_FA, O_QB, O_KB, O_VB, O_QI, O_KI, O_WI,
 O_QC, O_KC, O_VC, O_GC, O_QD, O_KD, O_VD, O_GATES, N_IN) = _OFFS
ZS_WI = 8


def _column_plan():
    perm = np.full((N_BLOCKS * LANES,), N_IN, np.int32)
    scale = np.ones((N_BLOCKS * LANES,), np.float32)

    def put(block, off, width, s=1.0):
        perm[block * LANES: block * LANES + width] = np.arange(off, off + width)
        scale[block * LANES: block * LANES + width] = s

    q_scale = HEAD_DIM ** -0.5
    q_scale_log2 = q_scale * LOG2_E
    put(CB_QA, O_QA, 512, q_scale_log2); put(CB_KA, O_KA, 512); put(CB_VA, O_VA, 512)
    put(CB_VC, O_VC, 512); put(CB_GC, O_GC, 512)
    for half in range(2):
        perm[CB_VB * LANES + half * 64: CB_VB * LANES + half * 64 + 64] = np.arange(O_VB, O_VB + 64)
        perm[CB_KB * LANES + half * 64: CB_KB * LANES + half * 64 + 64] = np.arange(O_KB, O_KB + 64)
        perm[CB_KI * LANES + half * 64: CB_KI * LANES + half * 64 + 64] = np.arange(O_KI, O_KI + 64)
        for g in range(SWA_KV_HEADS):
            perm[(CB_VD + g) * LANES + half * 64: (CB_VD + g) * LANES + half * 64 + 64] = np.arange(O_VD + 64 * g, O_VD + 64 * g + 64)
            perm[(CB_KD + g) * LANES + half * 64: (CB_KD + g) * LANES + half * 64 + 64] = np.arange(O_KD + 64 * g, O_KD + 64 * g + 64)
    put(CB_QB, O_QB, 512, q_scale_log2); put(CB_QC, O_QC, 512); put(CB_KC, O_KC, 512, q_scale)
    put(CB_QD, O_QD, 512, q_scale); put(CB_QI, O_QI, 256)
    return perm, scale


def _segment_offsets(perm, scale):
    offs, n = [], 0
    for _, width in _column_segments(perm, scale):
        offs.append(n)
        n += width
    return offs


def _column_segments(perm, scale):
    segments, n = [], 0
    while n < len(perm):
        e = n + 1
        if perm[n] == N_IN:
            while e < len(perm) and perm[e] == N_IN:
                e += 1
            segments.append((None, e - n))
        else:
            while e < len(perm) and perm[e] == perm[e - 1] + 1 and scale[e] == scale[n]:
                e += 1
            segments.append((int(perm[n]), e - n))
        n = e
    return segments


def _rmsnorm_f32(x, g):
    ms = jnp.mean(x * x, axis=-1, keepdims=True)
    return x * lax.rsqrt(ms + EPS) * g


def _lane_lt_half_tile(shape):
    return lax.broadcasted_iota(I32, shape, len(shape) - 1) < HEAD_DIM


def _dot_nt(a, b):
    return lax.dot_general(a, b, (((1,), (1,)), ((), ())), preferred_element_type=F32)


def _dot(a, b):
    return jnp.dot(a, b, preferred_element_type=F32)


def _repack_kernel(w_ref, main_ref, small_ref, gates_ref, *, segments):
    rows = w_ref.shape[0]
    off = 0
    for start, width, s in segments:
        if start is None:
            main_ref[:, off:off + width] = jnp.zeros((rows, width), BF16)
        else:
            main_ref[:, off:off + width] = (w_ref[:, start:start + width] * s).astype(BF16)
        off += width
    small_ref[:, 0:HEADS] = w_ref[:, O_FA:O_FA + HEADS].astype(BF16)
    small_ref[:, ZS_WI:ZS_WI + IDX_HEADS] = w_ref[:, O_WI:O_WI + IDX_HEADS].astype(BF16)
    small_ref[:, ZS_WI + IDX_HEADS:] = jnp.zeros((rows, LANES - ZS_WI - IDX_HEADS), BF16)
    gates_ref[...] = w_ref[:, O_GATES:N_IN].astype(BF16)


def _repack_w_in(w_in, layer):
    _, D, n_in = w_in.shape
    perm, scale = _column_plan()
    segments = tuple((start, width, float(scale[off]))
                     for (start, width), off in zip(_column_segments(perm, scale), _segment_offsets(perm, scale)))
    tr = 128
    n_main = N_BLOCKS * LANES
    n_gates = N_IN - O_GATES
    return pl.pallas_call(
        functools.partial(_repack_kernel, segments=segments),
        grid=(D // tr,),
        in_specs=[pl.BlockSpec((None, tr, n_in), lambda r: (layer, r, 0))],
        out_specs=[pl.BlockSpec((tr, n_main), lambda r: (r, 0)), pl.BlockSpec((tr, LANES), lambda r: (r, 0)),
                   pl.BlockSpec((tr, n_gates), lambda r: (r, 0))],
        out_shape=[jax.ShapeDtypeStruct((D, n_main), BF16), jax.ShapeDtypeStruct((D, LANES), BF16),
                   jax.ShapeDtypeStruct((D, n_gates), BF16)],
        compiler_params=pltpu.CompilerParams(dimension_semantics=("parallel",), vmem_limit_bytes=VMEM_LIMIT),
        name="repack",
    )(w_in)


def _inproj_kernel(x_ref, g_ref, w_ref, ws_ref, cos_ref, sin_ref, z_ref, zs_ref, *, tn, n_plain_tiles):
    tm = x_ref.shape[0]
    hb = _rmsnorm_f32(x_ref[...], g_ref[...]).astype(BF16)
    zs_ref[...] = _dot(hb, ws_ref[...])
    lane = lax.broadcasted_iota(I32, (tm, LANES), 1)
    first_half = (lane % HEAD_DIM) < HALF
    for j in range(w_ref.shape[1] // tn):
        acc = _dot(hb, w_ref[:, j * tn:(j + 1) * tn])
        if j < n_plain_tiles:
            z_ref[:, j * tn:(j + 1) * tn] = acc.astype(BF16)
            continue
        for c in range(tn // LANES):
            a = acc[:, c * LANES:(c + 1) * LANES]
            rot = jnp.where(first_half, pltpu.roll(a, LANES - HALF, 1), pltpu.roll(a, HALF, 1))
            col = j * tn + c * LANES
            z_ref[:, col:col + LANES] = (a * cos_ref[...] + rot * sin_ref[...]).astype(BF16)


def _inproj(x2, g, w_main, w_small, cos_t, sin_t, seq):
    T, D = x2.shape
    N = w_main.shape[1]
    tm = min(512, seq)
    tn = TN_IN
    pos_blocks = seq // tm
    const = lambda shape: pl.BlockSpec(shape, lambda i: (0,) * len(shape), pipeline_mode=pl.Buffered(1))
    return pl.pallas_call(
        functools.partial(_inproj_kernel, tn=tn, n_plain_tiles=N_PLAIN_BLOCKS * LANES // tn),
        grid=(T // tm,),
        in_specs=[
            pl.BlockSpec((tm, D), lambda i: (i, 0)),
            const((1, D)),
            const((D, N)),
            const((D, LANES)),
            pl.BlockSpec((tm, LANES), lambda i: (i % pos_blocks, 0)),
            pl.BlockSpec((tm, LANES), lambda i: (i % pos_blocks, 0)),
        ],
        out_specs=[
            pl.BlockSpec((tm, N), lambda i: (i, 0)),
            pl.BlockSpec((tm, LANES), lambda i: (i, 0)),
        ],
        out_shape=[jax.ShapeDtypeStruct((T, N), BF16), jax.ShapeDtypeStruct((T, LANES), F32)],
        compiler_params=pltpu.CompilerParams(dimension_semantics=("parallel",), vmem_limit_bytes=VMEM_LIMIT),
        name="inproj",
    )(x2, g, w_main, w_small, cos_t, sin_t)


def _split3(x):
    hi = x.astype(BF16)
    r1 = x - hi.astype(F32)
    mid = r1.astype(BF16)
    lo = (r1 - mid.astype(F32)).astype(BF16)
    return hi, mid, lo


def _fcum_kernel(zs_ref, bias_ref, tri_ref, place_ref, k_ref, kf_ref):
    n_chunks = zs_ref.shape[0] // LANES
    lo_half = _lane_lt_half_tile((LANES, LANES))

    def body(c, carry):
        rows = pl.ds(pl.multiple_of(c * LANES, LANES), LANES)
        xv = zs_ref[rows, :] + bias_ref[...]
        logf = jnp.minimum(xv, 0.0) - jnp.log(1.0 + jnp.exp(-jnp.abs(xv)))
        local = _dot(tri_ref[...], jnp.concatenate(_split3(logf), axis=1))
        cs = carry
        for j in range(F_PIECES):
            cs = cs + local[:, j * LANES:(j + 1) * LANES]
        placed = _dot(jnp.concatenate(_split3(cs * LOG2_E), axis=1), place_ref[...]).astype(BF16)
        for h in range(HEADS):
            kp = k_ref[rows, (h // 2) * LANES:(h // 2 + 1) * LANES]
            own = lo_half if h % 2 == 0 else jnp.logical_not(lo_half)
            kf_ref[rows, h * LANES:(h + 1) * LANES] = jnp.where(own, kp, placed)
        return cs[LANES - 1:LANES, :]

    lax.fori_loop(0, n_chunks, body, jnp.zeros((1, LANES), F32), unroll=FCUM_UNROLL)


def _f_piece_row(h):
    return (HEAD_DIM if h % 2 == 0 else 0) + F_PIECES * h


def _fcum(zs, bias_row, z, seq):
    T = zs.shape[0]
    tri = jnp.asarray(np.tril(np.ones((LANES, LANES), np.float32)), BF16)
    place = np.zeros((F_PIECES * LANES, LANES), np.float32)
    for h in range(HEADS):
        for j in range(F_PIECES):
            for half_start in (0, HEAD_DIM):
                place[j * LANES + h, half_start + F_PIECES * h + j] = 1.0
    return pl.pallas_call(
        _fcum_kernel,
        grid=(T // seq,),
        in_specs=[
            pl.BlockSpec((seq, LANES), lambda b: (b, 0)),
            pl.BlockSpec((1, LANES), lambda b: (0, 0)),
            pl.BlockSpec((LANES, LANES), lambda b: (0, 0)),
            pl.BlockSpec((F_PIECES * LANES, LANES), lambda b: (0, 0)),
            pl.BlockSpec((seq, BRANCH_WIDTH), lambda b: (b, CB_KA // 4)),
        ],
        out_specs=pl.BlockSpec((seq, HEADS * LANES), lambda b: (b, 0)),
        out_shape=jax.ShapeDtypeStruct((T, HEADS * LANES), BF16),
        compiler_params=pltpu.CompilerParams(dimension_semantics=("parallel",), vmem_limit_bytes=VMEM_LIMIT),
        name="fcum",
    )(zs, bias_row, tri, jnp.asarray(place, BF16), z)


def _flash_chunks(n_off, scores, values, sa_ref, sb_ref, m_ref, acc_ref):
    def produce(buf, kc, diagonal, h):
        buf[h] = scores(h, kc, diagonal)

    def consume(buf, kc, h):
        st = buf[h]
        m_old = m_ref[h]
        m_new = jnp.maximum(m_old, jnp.max(st, axis=0, keepdims=True))
        pt = jnp.exp2(st - m_new).astype(BF16)
        acc_ref[h] = jnp.exp2(m_old - m_new) * acc_ref[h] + _dot(values(h, kc), pt)
        m_ref[h] = m_new

    def consume_and_produce(cbuf, ckc, pbuf, pkc):
        for h in range(HEADS):
            produce(pbuf, pkc, False, h)
            consume(cbuf, ckc, h)

    def chunk_of_item(t):
        return jnp.where(t == 0, n_off, t - 1)

    for h in range(HEADS):
        produce(sa_ref, n_off, True, h)

    def pair(j, carry):
        consume_and_produce(sa_ref, chunk_of_item(2 * j), sb_ref, 2 * j)
        consume_and_produce(sb_ref, 2 * j, sa_ref, 2 * j + 1)
        return carry

    lax.fori_loop(0, n_off // 2, pair, 0)

    @pl.when(n_off % 2 == 0)
    def _():
        for h in range(HEADS):
            consume(sa_ref, chunk_of_item(n_off), h)

    @pl.when(n_off % 2 == 1)
    def _():
        consume_and_produce(sa_ref, chunk_of_item(n_off - 1), sb_ref, n_off - 1)
        for h in range(HEADS):
            consume(sb_ref, n_off - 1, h)


def _fox_kernel(q_ref, k_ref, v_ref, y_ref, vt_ref, qt_ref, sa_ref, sb_ref, m_ref, acc_ref, *, tq, tk):
    i = pl.program_id(1)
    n_kc = k_ref.shape[0] // tk
    pairs = HEADS // 2

    @pl.when(i == 0)
    def _():
        ones_row = jnp.where(lax.broadcasted_iota(I32, (VT_ROWS - HEAD_DIM, tk), 0) == 0, 1.0, 0.0).astype(BF16)
        for p in range(pairs):
            for c in range(n_kc):
                t = v_ref[c * tk:(c + 1) * tk, p * LANES:(p + 1) * LANES].astype(F32).T.astype(BF16)
                for e in range(2):
                    vt_ref[2 * p + e, c, 0:HEAD_DIM, :] = t[HEAD_DIM * e:HEAD_DIM * (e + 1), :]
                    vt_ref[2 * p + e, c, HEAD_DIM:VT_ROWS, :] = ones_row

    row = lax.broadcasted_iota(I32, (LANES, tq), 0)
    for p in range(pairs):
        qt = q_ref[:, p * LANES:(p + 1) * LANES].astype(F32).T
        for e in range(2):
            own = (row < HEAD_DIM) if e == 0 else (row >= HEAD_DIM)
            f0 = _f_piece_row(2 * p + e)
            minus_one = jnp.where((row >= f0) & (row < f0 + F_PIECES), -1.0, 0.0)
            qt_ref[2 * p + e] = jnp.where(own, qt, minus_one).astype(BF16)
    m_ref[...] = jnp.full(m_ref.shape, M_INIT, F32)
    acc_ref[...] = jnp.zeros(acc_ref.shape, F32)

    def scores(h, kc, diagonal):
        k0 = pl.multiple_of(kc * tk, tk)
        st = _dot(k_ref[pl.ds(k0, tk), h * LANES:(h + 1) * LANES], qt_ref[h])
        if diagonal:
            key_pos = k0 + lax.broadcasted_iota(I32, (tk, 1), 0)
            query_pos = i * tq + lax.broadcasted_iota(I32, (1, tq), 1)
            st = jnp.where(key_pos <= query_pos, st, MASKED)
        return st

    _flash_chunks(i, scores, lambda h, kc: vt_ref[h, kc], sa_ref, sb_ref, m_ref, acc_ref)

    for p in range(pairs):
        heads = []
        for h in (2 * p, 2 * p + 1):
            heads.append(acc_ref[h, 0:HEAD_DIM, :] / acc_ref[h, HEAD_DIM:HEAD_DIM + 1, :])
        y_ref[:, p * LANES:(p + 1) * LANES] = jnp.concatenate(heads, axis=0).T.astype(BF16)


def _fox(z, kf, batch, seq, tile):
    nq = seq // tile
    T = batch * seq
    return pl.pallas_call(
        functools.partial(_fox_kernel, tq=tile, tk=tile),
        grid=(batch, nq),
        in_specs=[
            pl.BlockSpec((tile, BRANCH_WIDTH), lambda b, i: (b * nq + i, CB_QA // 4)),
            pl.BlockSpec((seq, HEADS * LANES), lambda b, i: (b, 0)),
            pl.BlockSpec((seq, BRANCH_WIDTH), lambda b, i: (b, CB_VA // 4)),
        ],
        out_specs=pl.BlockSpec((tile, BRANCH_WIDTH), lambda b, i: (b * nq + i, 0)),
        out_shape=jax.ShapeDtypeStruct((T, BRANCH_WIDTH), BF16),
        scratch_shapes=[
            pltpu.VMEM((HEADS, seq // tile, VT_ROWS, tile), BF16),
            pltpu.VMEM((HEADS, LANES, tile), BF16),
            pltpu.VMEM((HEADS, tile, tile), F32),
            pltpu.VMEM((HEADS, tile, tile), F32),
            pltpu.VMEM((HEADS, 1, tile), F32),
            pltpu.VMEM((HEADS, VT_ROWS, tile), F32),
        ],
        compiler_params=pltpu.CompilerParams(dimension_semantics=("parallel", "arbitrary"), vmem_limit_bytes=VMEM_LIMIT),
        name="fox",
    )(z, kf, z)


def _half_rows_operand(qt, first_half):
    row = lax.broadcasted_iota(I32, qt.shape, 0)
    own = (row < HEAD_DIM) if first_half else (row >= HEAD_DIM)
    return jnp.where(own, qt, 0.0).astype(BF16)


def _count16(ref, n_chunks, pred):
    ck, tq = ref.shape[1:]

    def body(kc, acc):
        hit = jnp.where(pred(ref[kc]), jnp.int16(1), jnp.int16(0))
        parts = [hit[r * I16_ROWS:(r + 1) * I16_ROWS, :] for r in range(ck // I16_ROWS)]
        while len(parts) > 1:
            parts = [parts[n] + parts[n + 1] for n in range(0, len(parts) - 1, 2)] + parts[len(parts) & ~1:]
        return acc + parts[0]

    acc = lax.fori_loop(0, n_chunks, body, jnp.zeros((I16_ROWS, tq), I16))
    return jnp.sum(acc.astype(I32), axis=0, keepdims=True)


def _search16(ref, n_chunks, want):
    v0 = jnp.where(_count16(ref, n_chunks, lambda k: k >= 0) >= want, 0, I16_MIN).astype(I32)

    def bit_body(b, v):
        cand = v | jnp.left_shift(jnp.int32(1), 14 - b)
        c = _count16(ref, n_chunks, lambda k: k >= cand.astype(I16))
        return jnp.where(c >= want, cand, v)

    return lax.fori_loop(0, 15, bit_body, v0)


def _topk_threshold(key_ref, hi_ref, lo_ref, n_chunks, topk):
    hi_v = _search16(hi_ref, n_chunks, topk)
    want_lo = topk - _count16(hi_ref, n_chunks, lambda k: k > hi_v.astype(I16))

    def fill_lo(kc, carry):
        key = key_ref[kc]
        lo = (key & 0xFFFF) - 32768
        lo_ref[kc] = jnp.where((key >> 16) == hi_v, lo, I16_MIN).astype(I16)
        return carry

    lax.fori_loop(0, n_chunks, fill_lo, 0)
    lo_v = _search16(lo_ref, n_chunks, want_lo)
    need = want_lo - _count16(lo_ref, n_chunks, lambda k: k > lo_v.astype(I16))
    ties = _count16(lo_ref, n_chunks, lambda k: k == lo_v.astype(I16))
    split_tie = jnp.max(jnp.where(ties > need, 1, 0))
    v = hi_v * 65536 + (lo_v + 32768)
    return v, need.astype(F32), split_tie


def _topk_bias(key_ref, hi_ref, lo_ref, bias_ref, tril_ref, n_chunks, topk):
    tq = key_ref.shape[2]
    v, need, split_tie = _topk_threshold(key_ref, hi_ref, lo_ref, n_chunks, topk)

    @pl.when(split_tie != 0)
    def _():
        def bias_body(kc, carry):
            key = key_ref[kc]
            eq = (key == v) & (key > INT_MIN)
            eqf = jnp.where(eq, 1.0, 0.0)
            prefix = _dot(tril_ref[...], eqf.astype(BF16)) + carry
            sel = (key > v) | (eq & (prefix <= need))
            bias_ref[kc] = jnp.where(sel, 0.0, MASKED)
            return carry + jnp.sum(eqf, axis=0, keepdims=True)

        lax.fori_loop(0, n_chunks, bias_body, jnp.zeros((1, tq), F32))

    @pl.when(split_tie == 0)
    def _():
        floor = jnp.maximum(v, INT_MIN + 1)

        def bias_body(kc, carry):
            bias_ref[kc] = jnp.where(key_ref[kc] >= floor, 0.0, MASKED)
            return carry

        lax.fori_loop(0, n_chunks, bias_body, 0)


def _dsa_kernel(qi_ref, kk_ref, vv_ref, qb_ref, zs_ref, tril_ref, y_ref,
                key_ref, hi_ref, lo_ref, bias_ref, vt_ref, qit_ref, qbt_ref, sa_ref, sb_ref, m_ref, acc_ref, *, tq, ck, topk):
    i = pl.program_id(1)
    n_chunks = i + 1
    n_kc_total = kk_ref.shape[0] // ck

    @pl.when(i == 0)
    def _():
        ones_row = jnp.where(lax.broadcasted_iota(I32, (VT_ROWS - HEAD_DIM, ck), 0) == 0, 1.0, 0.0).astype(BF16)
        for c in range(n_kc_total):
            t = vv_ref[c * ck:(c + 1) * ck, :].astype(F32).T.astype(BF16)
            vt_ref[c, 0:HEAD_DIM, :] = t[0:HEAD_DIM, :]
            vt_ref[c, HEAD_DIM:VT_ROWS, :] = ones_row

    for p in range(IDX_HEADS // 2):
        qt = qi_ref[:, p * LANES:(p + 1) * LANES].astype(F32).T
        qit_ref[2 * p] = _half_rows_operand(qt, True)
        qit_ref[2 * p + 1] = _half_rows_operand(qt, False)
    for p in range(HEADS // 2):
        qt = qb_ref[:, p * LANES:(p + 1) * LANES].astype(F32).T
        qbt_ref[2 * p] = _half_rows_operand(qt, True)
        qbt_ref[2 * p + 1] = _half_rows_operand(qt, False)
    w_scale = (IDX_HEADS ** -0.5) * (HEAD_DIM ** -0.5)
    w_rows = zs_ref[...].T[ZS_WI:ZS_WI + IDX_HEADS, :] * w_scale

    def score_chunk(kc, diagonal):
        k0 = pl.multiple_of(kc * ck, ck)
        ki = kk_ref[pl.ds(k0, ck), LANES:2 * LANES]
        s = jnp.zeros((ck, tq), F32)
        for h in range(IDX_HEADS):
            s = s + jnp.maximum(_dot(ki, qit_ref[h]), 0.0) * w_rows[h:h + 1, :]
        s = jnp.where(s == 0.0, 0.0, s)
        bits = pltpu.bitcast(s, I32)
        key = bits ^ ((bits >> 31) & 0x7FFFFFFF)
        if diagonal:
            key_pos = k0 + lax.broadcasted_iota(I32, (ck, 1), 0)
            query_pos = i * tq + lax.broadcasted_iota(I32, (1, tq), 1)
            key = jnp.where(key_pos <= query_pos, key, INT_MIN)
        key_ref[kc] = key
        hi_ref[kc] = (key >> 16).astype(I16)

    def score_body(kc, carry):
        score_chunk(kc, False)
        return carry

    lax.fori_loop(0, i, score_body, 0)
    score_chunk(i, True)

    _topk_bias(key_ref, hi_ref, lo_ref, bias_ref, tril_ref, n_chunks, topk)

    m_ref[...] = jnp.full(m_ref.shape, M_INIT, F32)
    acc_ref[...] = jnp.zeros(acc_ref.shape, F32)

    def scores(h, kc, diagonal):
        k0 = pl.multiple_of(kc * ck, ck)
        return _dot(kk_ref[pl.ds(k0, ck), 0:LANES], qbt_ref[h]) + bias_ref[kc]

    _flash_chunks(i, scores, lambda h, kc: vt_ref[kc], sa_ref, sb_ref, m_ref, acc_ref)

    for p in range(HEADS // 2):
        heads = []
        for h in (2 * p, 2 * p + 1):
            heads.append(acc_ref[h, 0:HEAD_DIM, :] / acc_ref[h, HEAD_DIM:HEAD_DIM + 1, :])
        y_ref[:, p * LANES:(p + 1) * LANES] = jnp.concatenate(heads, axis=0).T.astype(BF16)


def _dsa(z, zs, batch, seq, tile):
    nq = seq // tile
    T = batch * seq
    topk = min(DSA_TOPK_MAX, seq // 4)
    tril = jnp.asarray(np.tril(np.ones((tile, tile), np.float32)), BF16)
    return pl.pallas_call(
        functools.partial(_dsa_kernel, tq=tile, ck=tile, topk=topk),
        grid=(batch, nq),
        in_specs=[
            pl.BlockSpec((tile, 2 * LANES), lambda b, i: (b * nq + i, CB_QI // 2)),
            pl.BlockSpec((seq, 2 * LANES), lambda b, i: (b, CB_KB // 2)),
            pl.BlockSpec((seq, LANES), lambda b, i: (b, CB_VB)),
            pl.BlockSpec((tile, BRANCH_WIDTH), lambda b, i: (b * nq + i, CB_QB // 4)),
            pl.BlockSpec((tile, LANES), lambda b, i: (b * nq + i, 0)),
            pl.BlockSpec((tile, tile), lambda b, i: (0, 0)),
        ],
        out_specs=pl.BlockSpec((tile, BRANCH_WIDTH), lambda b, i: (b * nq + i, 0)),
        out_shape=jax.ShapeDtypeStruct((T, BRANCH_WIDTH), BF16),
        scratch_shapes=[
            pltpu.VMEM((seq // tile, tile, tile), I32),
            pltpu.VMEM((seq // tile, tile, tile), I16),
            pltpu.VMEM((seq // tile, tile, tile), I16),
            pltpu.VMEM((seq // tile, tile, tile), F32),
            pltpu.VMEM((seq // tile, VT_ROWS, tile), BF16),
            pltpu.VMEM((IDX_HEADS, LANES, tile), BF16),
            pltpu.VMEM((HEADS, LANES, tile), BF16),
            pltpu.VMEM((HEADS, tile, tile), F32),
            pltpu.VMEM((HEADS, tile, tile), F32),
            pltpu.VMEM((HEADS, 1, tile), F32),
            pltpu.VMEM((HEADS, VT_ROWS, tile), F32),
        ],
        compiler_params=pltpu.CompilerParams(dimension_semantics=("parallel", "arbitrary"), vmem_limit_bytes=VMEM_LIMIT),
        name="dsa",
    )(z, z, z, z, zs, tril)


def _ret_kernel(q_ref, k_ref, v_ref, g_ref, dm_ref, xi_ref, zeta_ref, decay_ref, gain_ref, y_ref):
    C = RET_CHUNK
    n_chunks = q_ref.shape[0] // C
    lo_half = _lane_lt_half_tile((C, LANES))
    decay = decay_ref[0]
    block_diag = jnp.where(decay > 0.0, 1.0, 0.0)

    def head_stat(x):
        s_lo = jnp.sum(jnp.where(lo_half, x, 0.0), axis=-1, keepdims=True)
        s_hi = jnp.sum(jnp.where(lo_half, 0.0, x), axis=-1, keepdims=True)
        return jnp.where(lo_half, s_lo, s_hi) * (1.0 / HEAD_DIM)

    def group(gi, state):
        rows = [pl.ds(pl.multiple_of((gi * RET_GROUP + c) * C, C), C) for c in range(RET_GROUP)]
        q = [q_ref[r, :] for r in rows]
        k = [k_ref[r, :] for r in rows]
        v = [v_ref[r, :] for r in rows]
        qk, upd = [], []
        for c in range(RET_GROUP):
            zero = jnp.zeros_like(q[c])
            heads = (jnp.where(lo_half, q[c], zero), jnp.where(lo_half, zero, q[c]))
            qk.append([_dot_nt(heads[e], k[c]) for e in range(2)])
            kz = (k[c].astype(F32) * zeta_ref[0]).T.astype(BF16)
            upd.append(_dot(kz, v[c]) * block_diag)
        cross = []
        for c in range(RET_GROUP):
            cross.append(_dot(q[c], state.astype(BF16)) * xi_ref[0])
            state = state * decay + upd[c]
        for c in range(RET_GROUP):
            inner = [_dot((qk[c][e] * dm_ref[e]).astype(BF16), v[c]) for e in range(2)]
            r = jnp.where(lo_half, inner[0], inner[1]) + cross[c]
            mu = head_stat(r)
            d = r - mu
            var = head_stat(d * d)
            y = d * lax.rsqrt(var + EPS) * gain_ref[...]
            gate = g_ref[rows[c], :].astype(F32)
            y_ref[rows[c], :] = (y * (gate * jax.nn.sigmoid(gate))).astype(BF16)
        return state

    lax.fori_loop(0, n_chunks // RET_GROUP, group, jnp.zeros((LANES, LANES), F32))


def _retention_tables():
    C = RET_CHUNK
    log_g = np.log(1.0 - 2.0 ** (-5.0 - np.arange(HEADS, dtype=np.float64)))
    n = np.arange(C, dtype=np.float64)
    diff = n[:, None] - n[None, :]
    dm = np.where(diff[None] >= 0, np.exp(diff[None] * log_g[:, None, None]), 0.0)
    xi = np.exp((n + 1.0)[None, :] * log_g[:, None])
    zeta = np.exp((C - 1.0 - n)[None, :] * log_g[:, None])
    g_c = np.exp(C * log_g)
    pair = lambda t: np.repeat(t.reshape(HEADS // 2, 2, C), HEAD_DIM, axis=1).transpose(0, 2, 1)
    decay = np.zeros((HEADS // 2, LANES, LANES))
    for p in range(HEADS // 2):
        decay[p, :HEAD_DIM, :HEAD_DIM] = g_c[2 * p]
        decay[p, HEAD_DIM:, HEAD_DIM:] = g_c[2 * p + 1]
    f = lambda t: jnp.asarray(t.astype(np.float32))
    return f(dm), f(pair(xi)), f(pair(zeta)), f(decay)


def _retention(z, gain_row, batch, seq):
    T = batch * seq
    C = RET_CHUNK
    dm, xi, zeta, decay = _retention_tables()
    pairs = HEADS // 2
    return pl.pallas_call(
        _ret_kernel,
        grid=(batch, pairs),
        in_specs=[
            pl.BlockSpec((seq, LANES), lambda b, p: (b, CB_QC + p)),
            pl.BlockSpec((seq, LANES), lambda b, p: (b, CB_KC + p)),
            pl.BlockSpec((seq, LANES), lambda b, p: (b, CB_VC + p)),
            pl.BlockSpec((seq, LANES), lambda b, p: (b, CB_GC + p)),
            pl.BlockSpec((2, C, C), lambda b, p: (p, 0, 0)),
            pl.BlockSpec((1, C, LANES), lambda b, p: (p, 0, 0)),
            pl.BlockSpec((1, C, LANES), lambda b, p: (p, 0, 0)),
            pl.BlockSpec((1, LANES, LANES), lambda b, p: (p, 0, 0)),
            pl.BlockSpec((1, LANES), lambda b, p: (0, p)),
        ],
        out_specs=pl.BlockSpec((seq, LANES), lambda b, p: (b, p)),
        out_shape=jax.ShapeDtypeStruct((T, BRANCH_WIDTH), BF16),
        compiler_params=pltpu.CompilerParams(dimension_semantics=("parallel", "parallel")),
        name="retention",
    )(z, z, z, z, dm, xi, zeta, decay, gain_row)


def _swa_kernel(q_ref, kc_ref, kp_ref, vc_ref, vp_ref, sink_ref, y_ref, *, blocks):
    n = pl.program_id(1)
    W = SWA_WINDOW
    G = HEADS // SWA_KV_HEADS
    lo_half = _lane_lt_half_tile((W, LANES))
    qi = lax.broadcasted_iota(I32, (W, 2 * W), 0)
    kj = lax.broadcasted_iota(I32, (W, 2 * W), 1)
    in_band = (kj > qi) & (kj <= qi + W)
    chains = [(j, g) for j in range(blocks) for g in range(SWA_KV_HEADS)]

    def band_of(ref_prev, ref_cur, j, cols):
        if j == 0:
            return jnp.concatenate([ref_prev[:, cols], ref_cur[0:W, cols]], axis=0)
        return ref_cur[(j - 1) * W:(j + 1) * W, cols]

    sinks = {}
    for g in range(SWA_KV_HEADS):
        sinks[g] = jnp.concatenate(
            [jnp.broadcast_to(sink_ref[:, G * g + r:G * g + r + 1], (W, 1)) for r in range(G)], axis=0)

    scores = {}
    for j, g in chains:
        cols = slice(g * LANES, (g + 1) * LANES)
        q_rows = []
        for r in range(G):
            h = G * g + r
            t = q_ref[j * W:(j + 1) * W, (h // 2) * LANES:(h // 2 + 1) * LANES]
            keep = lo_half if h % 2 == 0 else jnp.logical_not(lo_half)
            q_rows.append(jnp.where(keep, t, jnp.zeros_like(t)))
        s = _dot_nt(jnp.concatenate(q_rows, axis=0), band_of(kp_ref, kc_ref, j, cols)).reshape(G, W, 2 * W)
        band = in_band & ((kj >= W) | (n > 0)) if j == 0 else in_band
        scores[j, g] = jnp.where(band[None], s, MASKED).reshape(G * W, 2 * W)

    probs, sink_terms = {}, {}
    for j, g in chains:
        s = scores[j, g]
        m = jnp.maximum(jnp.max(s, axis=-1, keepdims=True), sinks[g])
        probs[j, g] = jnp.exp(s - m).astype(BF16)
        sink_terms[j, g] = jnp.exp(sinks[g] - m)

    ones = jnp.ones((2 * W, LANES), BF16)
    for j, g in chains:
        cols = slice(g * LANES, (g + 1) * LANES)
        denom = _dot(probs[j, g], ones) + sink_terms[j, g]
        o = _dot(probs[j, g], band_of(vp_ref, vc_ref, j, cols)) / denom
        for r2 in range(G // 2):
            even = o[(2 * r2) * W:(2 * r2 + 1) * W, :]
            odd = o[(2 * r2 + 1) * W:(2 * r2 + 2) * W, :]
            pcol = (G * g) // 2 + r2
            y_ref[j * W:(j + 1) * W, pcol * LANES:(pcol + 1) * LANES] = jnp.where(lo_half, even, odd).astype(BF16)


def _swa(z, sink_row, batch, seq):
    T = batch * seq
    W = SWA_WINDOW
    blocks = min(SWA_BLOCKS, seq // W)
    rows = blocks * W
    ns = seq // rows
    cur = lambda cb: (lambda b, n: (b * ns + n, cb))
    prev = lambda cb: (lambda b, n: (jnp.maximum((b * ns + n) * blocks - 1, b * ns * blocks), cb))
    return pl.pallas_call(
        functools.partial(_swa_kernel, blocks=blocks),
        grid=(batch, ns),
        in_specs=[
            pl.BlockSpec((rows, BRANCH_WIDTH), cur(CB_QD // 4)),
            pl.BlockSpec((rows, 2 * LANES), cur(CB_KD // 2)),
            pl.BlockSpec((W, 2 * LANES), prev(CB_KD // 2)),
            pl.BlockSpec((rows, 2 * LANES), cur(CB_VD // 2)),
            pl.BlockSpec((W, 2 * LANES), prev(CB_VD // 2)),
            pl.BlockSpec((1, LANES), lambda b, n: (0, 0)),
        ],
        out_specs=pl.BlockSpec((rows, BRANCH_WIDTH), lambda b, n: (b * ns + n, 0)),
        out_shape=jax.ShapeDtypeStruct((T, BRANCH_WIDTH), BF16),
        compiler_params=pltpu.CompilerParams(dimension_semantics=("parallel", "arbitrary"), vmem_limit_bytes=VMEM_LIMIT),
        name="swa",
    )(z, z, z, z, z, sink_row)


def _merge_kernel(x_ref, g_ref, ya_ref, yb_ref, yc_ref, yd_ref, wg_ref, wb_ref, wo_ref, o_ref):
    x = x_ref[...]
    D = x.shape[1]
    hb = _rmsnorm_f32(x, g_ref[...]).astype(BF16)
    merged = jnp.zeros(x.shape, F32)
    for n, y_ref in enumerate((ya_ref, yb_ref, yc_ref, yd_ref)):
        gate = jax.nn.sigmoid(_dot(hb, wg_ref[:, n * D:(n + 1) * D]))
        merged = merged + gate * _dot(y_ref[...], wb_ref[n])
    o_ref[...] = x + _dot(merged.astype(BF16), wo_ref[...])


def _merge(x2, g, ys, w_gates, w_branch, w_out):
    T, D = x2.shape
    tm = min(512, T)
    const = lambda shape: pl.BlockSpec(shape, lambda i: (0,) * len(shape), pipeline_mode=pl.Buffered(1))
    y_spec = pl.BlockSpec((tm, BRANCH_WIDTH), lambda i: (i, 0))
    return pl.pallas_call(
        _merge_kernel,
        grid=(T // tm,),
        in_specs=[
            pl.BlockSpec((tm, D), lambda i: (i, 0)),
            const((1, D)),
            y_spec, y_spec, y_spec, y_spec,
            const((D, N_BRANCH * D)),
            const((N_BRANCH, BRANCH_WIDTH, D)),
            const((D, D)),
        ],
        out_specs=pl.BlockSpec((tm, D), lambda i: (i, 0)),
        out_shape=jax.ShapeDtypeStruct((T, D), F32),
        compiler_params=pltpu.CompilerParams(dimension_semantics=("parallel",), vmem_limit_bytes=VMEM_LIMIT),
        name="merge",
    )(x2, g, *ys, w_gates, w_branch, w_out)


def _ffn_kernel(x_ref, g_ref, wup_ref, cw_ref, cb_ref, wdn_ref, fg_ref, o_ref, carry_ref, u_ref, act_ref,
                *, tc, final_norm):
    n = pl.program_id(1)
    tm = x_ref.shape[0]
    d_ff = wdn_ref.shape[0]
    PAD = 8
    n_slots = u_ref.shape[0]

    @pl.when(n == 0)
    def _():
        carry_ref[...] = jnp.zeros_like(carry_ref)

    x = x_ref[...]
    hb = _rmsnorm_f32(x, g_ref[...]).astype(BF16)

    def conv(col0, slot):
        cols = slice(col0, col0 + tc)
        u = _dot(hb, wup_ref[:, cols])
        u_ref[slot, 0:PAD, :] = carry_ref[:, cols]
        u_ref[slot, PAD:PAD + tm, :] = u
        carry_ref[:, cols] = u[tm - PAD:tm, :]
        w = cw_ref[:, cols]
        return (w[2:3, :] * u + w[1:2, :] * u_ref[slot, PAD - 1:PAD - 1 + tm, :]
                + w[0:1, :] * u_ref[slot, PAD - 2:PAD - 2 + tm, :] + cb_ref[:, cols])

    for c in range(d_ff // tc):
        a = conv(c * tc, (2 * c) % n_slots)
        b = conv(d_ff + c * tc, (2 * c + 1) % n_slots)
        act_ref[:, c * tc:(c + 1) * tc] = (a * jax.nn.sigmoid(a) * b).astype(BF16)
    y = x + _dot(act_ref[...], wdn_ref[...])
    if final_norm:
        y = _rmsnorm_f32(y, fg_ref[...])
    o_ref[...] = y


def _ffn(x2, g, w_up, conv_w, conv_b, w_down, final_g, batch, seq, final_norm):
    T, D = x2.shape
    d_ff = w_down.shape[0]
    tm = min(512, seq)
    tc = 256
    nb = seq // tm
    const = lambda shape: pl.BlockSpec(shape, lambda b, n: (0,) * len(shape), pipeline_mode=pl.Buffered(1))
    return pl.pallas_call(
        functools.partial(_ffn_kernel, tc=tc, final_norm=final_norm),
        grid=(batch, nb),
        in_specs=[
            pl.BlockSpec((tm, D), lambda b, n: (b * nb + n, 0)),
            const((1, D)),
            const((D, 2 * d_ff)),
            const((CONV_WIDTH, 2 * d_ff)),
            const((1, 2 * d_ff)),
            const((d_ff, D)),
            const((1, D)),
        ],
        out_specs=pl.BlockSpec((tm, D), lambda b, n: (b * nb + n, 0)),
        out_shape=jax.ShapeDtypeStruct((T, D), F32),
        scratch_shapes=[pltpu.VMEM((8, 2 * d_ff), F32), pltpu.VMEM((FFN_SLOTS, tm + 8, tc), F32),
                        pltpu.VMEM((tm, d_ff), BF16)],
        compiler_params=pltpu.CompilerParams(dimension_semantics=("parallel", "arbitrary"), vmem_limit_bytes=VMEM_LIMIT),
        name="ffn",
    )(x2, g, w_up, conv_w, conv_b, w_down, final_g)


def _rope_tables(seq):
    pos = jnp.arange(seq, dtype=F32)
    inv = 1.0 / (ROPE_THETA ** (jnp.arange(HALF, dtype=F32) * 2.0 / HEAD_DIM))
    ang = pos[:, None] * inv[None, :]
    cos, sin = jnp.cos(ang), jnp.sin(ang)
    cos_t = jnp.tile(cos, (1, LANES // HALF))
    sin_t = jnp.tile(jnp.concatenate([-sin, sin], axis=1), (1, LANES // HEAD_DIM))
    return cos_t, sin_t


def kernel(x, attn_norm, w_in, forget_bias, ret_norm, attn_sinks, w_branch, w_out, ffn_norm, w_up, conv_w, conv_b, w_down, final_norm):
    B, S, D = x.shape
    depth = w_in.shape[0]
    T = B * S
    cos_t, sin_t = _rope_tables(S)
    tile_fox = min(256, S)
    tile_dsa = min(256, S)
    pad_row = lambda v: jnp.pad(v.astype(F32), (0, LANES - v.shape[0]))[None, :]

    x2 = x.reshape(T, D)
    for l in range(depth):
        w_main, w_small, w_gates = _repack_w_in(w_in, l)

        z, zs = _inproj(x2, attn_norm[l][None, :], w_main, w_small, cos_t, sin_t, S)
        kf = _fcum(zs, pad_row(forget_bias[l]), z, S)

        y_a = _fox(z, kf, B, S, tile_fox)
        y_b = _dsa(z, zs, B, S, tile_dsa)
        y_c = _retention(z, ret_norm[l][None, :].astype(F32), B, S)
        y_d = _swa(z, pad_row(attn_sinks[l]), B, S)

        x2 = _merge(x2, attn_norm[l][None, :], (y_a, y_b, y_c, y_d), w_gates,
                    w_branch[l].astype(BF16), w_out[l].astype(BF16))
        x2 = _ffn(x2, ffn_norm[l][None, :], w_up[l].astype(BF16), conv_w[l], conv_b[l][None, :],
                  w_down[l].astype(BF16), final_norm[None, :], B, S, final_norm=(l == depth - 1))
    return x2.reshape(B, S, D)
```

```python
import functools

import numpy as np
import jax
import jax.numpy as jnp
from jax import lax
from jax.experimental import pallas as pl
from jax.experimental.pallas import tpu as pltpu

F32 = jnp.float32
BF16 = jnp.bfloat16
I32 = jnp.int32
SUBLANES = 8
WORD_BITS = 32

LANES = 128
HEAD_DIM = 64
HALF = HEAD_DIM // 2
HEADS = 8
BRANCH_WIDTH = HEADS * HEAD_DIM
IDX_HEADS = 4
SWA_KV_HEADS = 2
SWA_WINDOW = 128
RET_CHUNK = 128
RET_GROUP = 4
SWA_BLOCKS = 4
FFN_SLOTS = 4
FCUM_UNROLL = 4
N_BRANCH = 4
CONV_WIDTH = 3
DSA_TOPK_MAX = 256
ROPE_THETA = 10000.0
EPS = 1e-6
LOG2_E = 1.4426950408889634
VMEM_LIMIT = 56 * 1024 * 1024

VT_ROWS = HEAD_DIM + 16
F_PIECES = 3
M_INIT = -1e30
MASKED = -2e30
INT_MIN = -2 ** 31

CB_QA, CB_KA, CB_VA, CB_VC, CB_GC = 0, 4, 8, 12, 16
CB_VB, CB_VD = 20, 22
N_PLAIN_BLOCKS = 24
CB_QB, CB_QC, CB_KC, CB_QD, CB_QI = 24, 28, 32, 36, 40
CB_KB, CB_KI, CB_KD = 42, 43, 44
N_BLOCKS = 46
TN_IN = 256

_SIZES = (512, 512, 512, 8, 512, 64, 64, 256, 64, 4, 512, 512, 512, 512, 512, 128, 128, 4096)
_OFFS = np.concatenate([[0], np.cumsum(_SIZES)]).tolist()
(O_QA, O_KA, O_VA, O_FA, O_QB, O_KB, O_VB, O_QI, O_KI, O_WI,
 O_QC, O_KC, O_VC, O_GC, O_QD, O_KD, O_VD, O_GATES, N_IN) = _OFFS
ZS_WI = 8


def _column_plan():
    perm = np.full((N_BLOCKS * LANES,), N_IN, np.int32)
    scale = np.ones((N_BLOCKS * LANES,), np.float32)

    def put(block, off, width, s=1.0):
        perm[block * LANES: block * LANES + width] = np.arange(off, off + width)
        scale[block * LANES: block * LANES + width] = s

    q_scale = HEAD_DIM ** -0.5
    q_scale_log2 = q_scale * LOG2_E
    put(CB_QA, O_QA, 512, q_scale_log2); put(CB_KA, O_KA, 512); put(CB_VA, O_VA, 512)
    put(CB_VC, O_VC, 512); put(CB_GC, O_GC, 512)
    for half in range(2):
        perm[CB_VB * LANES + half * 64: CB_VB * LANES + half * 64 + 64] = np.arange(O_VB, O_VB + 64)
        perm[CB_KB * LANES + half * 64: CB_KB * LANES + half * 64 + 64] = np.arange(O_KB, O_KB + 64)
        perm[CB_KI * LANES + half * 64: CB_KI * LANES + half * 64 + 64] = np.arange(O_KI, O_KI + 64)
        for g in range(SWA_KV_HEADS):
            perm[(CB_VD + g) * LANES + half * 64: (CB_VD + g) * LANES + half * 64 + 64] = np.arange(O_VD + 64 * g, O_VD + 64 * g + 64)
            perm[(CB_KD + g) * LANES + half * 64: (CB_KD + g) * LANES + half * 64 + 64] = np.arange(O_KD + 64 * g, O_KD + 64 * g + 64)
    put(CB_QB, O_QB, 512, q_scale_log2); put(CB_QC, O_QC, 512); put(CB_KC, O_KC, 512, q_scale)
    put(CB_QD, O_QD, 512, q_scale); put(CB_QI, O_QI, 256)
    return perm, scale


def _segment_offsets(perm, scale):
    offs, n = [], 0
    for _, width in _column_segments(perm, scale):
        offs.append(n)
        n += width
    return offs


def _column_segments(perm, scale):
    segments, n = [], 0
    while n < len(perm):
        e = n + 1
        if perm[n] == N_IN:
            while e < len(perm) and perm[e] == N_IN:
                e += 1
            segments.append((None, e - n))
        else:
            while e < len(perm) and perm[e] == perm[e - 1] + 1 and scale[e] == scale[n]:
                e += 1
            segments.append((int(perm[n]), e - n))
        n = e
    return segments


def _rmsnorm_f32(x, g):
    ms = jnp.mean(x * x, axis=-1, keepdims=True)
    return x * lax.rsqrt(ms + EPS) * g


def _lane_lt_half_tile(shape):
    return lax.broadcasted_iota(I32, shape, len(shape) - 1) < HEAD_DIM


def _dot_nt(a, b):
    return lax.dot_general(a, b, (((1,), (1,)), ((), ())), preferred_element_type=F32)


def _dot(a, b):
    return jnp.dot(a, b, preferred_element_type=F32)


def _repack_kernel(w_ref, main_ref, small_ref, gates_ref, *, segments):
    rows = w_ref.shape[0]
    off = 0
    for start, width, s in segments:
        if start is None:
            main_ref[:, off:off + width] = jnp.zeros((rows, width), BF16)
        else:
            main_ref[:, off:off + width] = (w_ref[:, start:start + width] * s).astype(BF16)
        off += width
    small_ref[:, 0:HEADS] = w_ref[:, O_FA:O_FA + HEADS].astype(BF16)
    small_ref[:, ZS_WI:ZS_WI + IDX_HEADS] = w_ref[:, O_WI:O_WI + IDX_HEADS].astype(BF16)
    small_ref[:, ZS_WI + IDX_HEADS:] = jnp.zeros((rows, LANES - ZS_WI - IDX_HEADS), BF16)
    gates_ref[...] = w_ref[:, O_GATES:N_IN].astype(BF16)


def _repack_w_in(w_in, layer):
    _, D, n_in = w_in.shape
    perm, scale = _column_plan()
    segments = tuple((start, width, float(scale[off]))
                     for (start, width), off in zip(_column_segments(perm, scale), _segment_offsets(perm, scale)))
    tr = 128
    n_main = N_BLOCKS * LANES
    n_gates = N_IN - O_GATES
    return pl.pallas_call(
        functools.partial(_repack_kernel, segments=segments),
        grid=(D // tr,),
        in_specs=[pl.BlockSpec((None, tr, n_in), lambda r: (layer, r, 0))],
        out_specs=[pl.BlockSpec((tr, n_main), lambda r: (r, 0)), pl.BlockSpec((tr, LANES), lambda r: (r, 0)),
                   pl.BlockSpec((tr, n_gates), lambda r: (r, 0))],
        out_shape=[jax.ShapeDtypeStruct((D, n_main), BF16), jax.ShapeDtypeStruct((D, LANES), BF16),
                   jax.ShapeDtypeStruct((D, n_gates), BF16)],
        compiler_params=pltpu.CompilerParams(dimension_semantics=("parallel",), vmem_limit_bytes=VMEM_LIMIT),
        name="repack",
    )(w_in)


def _inproj_kernel(x_ref, g_ref, w_ref, ws_ref, cos_ref, sin_ref, z_ref, zs_ref, *, tn, n_plain_tiles):
    tm = x_ref.shape[0]
    hb = _rmsnorm_f32(x_ref[...], g_ref[...]).astype(BF16)
    zs_ref[...] = _dot(hb, ws_ref[...])
    lane = lax.broadcasted_iota(I32, (tm, LANES), 1)
    first_half = (lane % HEAD_DIM) < HALF
    for j in range(w_ref.shape[1] // tn):
        acc = _dot(hb, w_ref[:, j * tn:(j + 1) * tn])
        if j < n_plain_tiles:
            z_ref[:, j * tn:(j + 1) * tn] = acc.astype(BF16)
            continue
        for c in range(tn // LANES):
            a = acc[:, c * LANES:(c + 1) * LANES]
            rot = jnp.where(first_half, pltpu.roll(a, LANES - HALF, 1), pltpu.roll(a, HALF, 1))
            col = j * tn + c * LANES
            z_ref[:, col:col + LANES] = (a * cos_ref[...] + rot * sin_ref[...]).astype(BF16)


def _inproj(x2, g, w_main, w_small, cos_t, sin_t, seq):
    T, D = x2.shape
    N = w_main.shape[1]
    tm = min(512, seq)
    tn = TN_IN
    pos_blocks = seq // tm
    const = lambda shape: pl.BlockSpec(shape, lambda i: (0,) * len(shape), pipeline_mode=pl.Buffered(1))
    return pl.pallas_call(
        functools.partial(_inproj_kernel, tn=tn, n_plain_tiles=N_PLAIN_BLOCKS * LANES // tn),
        grid=(T // tm,),
        in_specs=[
            pl.BlockSpec((tm, D), lambda i: (i, 0)),
            const((1, D)),
            const((D, N)),
            const((D, LANES)),
            pl.BlockSpec((tm, LANES), lambda i: (i % pos_blocks, 0)),
            pl.BlockSpec((tm, LANES), lambda i: (i % pos_blocks, 0)),
        ],
        out_specs=[
            pl.BlockSpec((tm, N), lambda i: (i, 0)),
            pl.BlockSpec((tm, LANES), lambda i: (i, 0)),
        ],
        out_shape=[jax.ShapeDtypeStruct((T, N), BF16), jax.ShapeDtypeStruct((T, LANES), F32)],
        compiler_params=pltpu.CompilerParams(dimension_semantics=("parallel",), vmem_limit_bytes=VMEM_LIMIT),
        name="inproj",
    )(x2, g, w_main, w_small, cos_t, sin_t)


def _split3(x):
    hi = x.astype(BF16)
    r1 = x - hi.astype(F32)
    mid = r1.astype(BF16)
    lo = (r1 - mid.astype(F32)).astype(BF16)
    return hi, mid, lo


def _fcum_kernel(zs_ref, bias_ref, tri_ref, place_ref, k_ref, kf_ref):
    n_chunks = zs_ref.shape[0] // LANES
    lo_half = _lane_lt_half_tile((LANES, LANES))

    def body(c, carry):
        rows = pl.ds(pl.multiple_of(c * LANES, LANES), LANES)
        xv = zs_ref[rows, :] + bias_ref[...]
        logf = jnp.minimum(xv, 0.0) - jnp.log(1.0 + jnp.exp(-jnp.abs(xv)))
        local = _dot(tri_ref[...], jnp.concatenate(_split3(logf), axis=1))
        cs = carry
        for j in range(F_PIECES):
            cs = cs + local[:, j * LANES:(j + 1) * LANES]
        placed = _dot(jnp.concatenate(_split3(cs * LOG2_E), axis=1), place_ref[...]).astype(BF16)
        for h in range(HEADS):
            kp = k_ref[rows, (h // 2) * LANES:(h // 2 + 1) * LANES]
            own = lo_half if h % 2 == 0 else jnp.logical_not(lo_half)
            kf_ref[rows, h * LANES:(h + 1) * LANES] = jnp.where(own, kp, placed)
        return cs[LANES - 1:LANES, :]

    lax.fori_loop(0, n_chunks, body, jnp.zeros((1, LANES), F32), unroll=FCUM_UNROLL)


def _f_piece_row(h):
    return (HEAD_DIM if h % 2 == 0 else 0) + F_PIECES * h


def _fcum(zs, bias_row, z, seq):
    T = zs.shape[0]
    tri = jnp.asarray(np.tril(np.ones((LANES, LANES), np.float32)), BF16)
    place = np.zeros((F_PIECES * LANES, LANES), np.float32)
    for h in range(HEADS):
        for j in range(F_PIECES):
            for half_start in (0, HEAD_DIM):
                place[j * LANES + h, half_start + F_PIECES * h + j] = 1.0
    return pl.pallas_call(
        _fcum_kernel,
        grid=(T // seq,),
        in_specs=[
            pl.BlockSpec((seq, LANES), lambda b: (b, 0)),
            pl.BlockSpec((1, LANES), lambda b: (0, 0)),
            pl.BlockSpec((LANES, LANES), lambda b: (0, 0)),
            pl.BlockSpec((F_PIECES * LANES, LANES), lambda b: (0, 0)),
            pl.BlockSpec((seq, BRANCH_WIDTH), lambda b: (b, CB_KA // 4)),
        ],
        out_specs=pl.BlockSpec((seq, HEADS * LANES), lambda b: (b, 0)),
        out_shape=jax.ShapeDtypeStruct((T, HEADS * LANES), BF16),
        compiler_params=pltpu.CompilerParams(dimension_semantics=("parallel",), vmem_limit_bytes=VMEM_LIMIT),
        name="fcum",
    )(zs, bias_row, tri, jnp.asarray(place, BF16), z)


def _flash_chunks(n_off, scores, values, sa_ref, sb_ref, m_ref, acc_ref):
    def produce(buf, kc, diagonal, h):
        buf[h] = scores(h, kc, diagonal)

    def consume(buf, kc, h):
        st = buf[h]
        m_old = m_ref[h]
        m_new = jnp.maximum(m_old, jnp.max(st, axis=0, keepdims=True))
        pt = jnp.exp2(st - m_new).astype(BF16)
        acc_ref[h] = jnp.exp2(m_old - m_new) * acc_ref[h] + _dot(values(h, kc), pt)
        m_ref[h] = m_new

    def consume_and_produce(cbuf, ckc, pbuf, pkc):
        for h in range(HEADS):
            produce(pbuf, pkc, False, h)
            consume(cbuf, ckc, h)

    def chunk_of_item(t):
        return jnp.where(t == 0, n_off, t - 1)

    for h in range(HEADS):
        produce(sa_ref, n_off, True, h)

    def pair(j, carry):
        consume_and_produce(sa_ref, chunk_of_item(2 * j), sb_ref, 2 * j)
        consume_and_produce(sb_ref, 2 * j, sa_ref, 2 * j + 1)
        return carry

    lax.fori_loop(0, n_off // 2, pair, 0)

    @pl.when(n_off % 2 == 0)
    def _():
        for h in range(HEADS):
            consume(sa_ref, chunk_of_item(n_off), h)

    @pl.when(n_off % 2 == 1)
    def _():
        consume_and_produce(sa_ref, chunk_of_item(n_off - 1), sb_ref, n_off - 1)
        for h in range(HEADS):
            consume(sb_ref, n_off - 1, h)


def _fox_kernel(q_ref, k_ref, v_ref, y_ref, vt_ref, qt_ref, sa_ref, sb_ref, m_ref, acc_ref, *, tq, tk):
    i = pl.program_id(1)
    n_kc = k_ref.shape[0] // tk
    pairs = HEADS // 2

    @pl.when(i == 0)
    def _():
        ones_row = jnp.where(lax.broadcasted_iota(I32, (VT_ROWS - HEAD_DIM, tk), 0) == 0, 1.0, 0.0).astype(BF16)
        for p in range(pairs):
            for c in range(n_kc):
                t = v_ref[c * tk:(c + 1) * tk, p * LANES:(p + 1) * LANES].astype(F32).T.astype(BF16)
                for e in range(2):
                    vt_ref[2 * p + e, c, 0:HEAD_DIM, :] = t[HEAD_DIM * e:HEAD_DIM * (e + 1), :]
                    vt_ref[2 * p + e, c, HEAD_DIM:VT_ROWS, :] = ones_row

    row = lax.broadcasted_iota(I32, (LANES, tq), 0)
    for p in range(pairs):
        qt = q_ref[:, p * LANES:(p + 1) * LANES].astype(F32).T
        for e in range(2):
            own = (row < HEAD_DIM) if e == 0 else (row >= HEAD_DIM)
            f0 = _f_piece_row(2 * p + e)
            minus_one = jnp.where((row >= f0) & (row < f0 + F_PIECES), -1.0, 0.0)
            qt_ref[2 * p + e] = jnp.where(own, qt, minus_one).astype(BF16)
    m_ref[...] = jnp.full(m_ref.shape, M_INIT, F32)
    acc_ref[...] = jnp.zeros(acc_ref.shape, F32)

    def scores(h, kc, diagonal):
        k0 = pl.multiple_of(kc * tk, tk)
        st = _dot(k_ref[pl.ds(k0, tk), h * LANES:(h + 1) * LANES], qt_ref[h])
        if diagonal:
            key_pos = k0 + lax.broadcasted_iota(I32, (tk, 1), 0)
            query_pos = i * tq + lax.broadcasted_iota(I32, (1, tq), 1)
            st = jnp.where(key_pos <= query_pos, st, MASKED)
        return st

    _flash_chunks(i, scores, lambda h, kc: vt_ref[h, kc], sa_ref, sb_ref, m_ref, acc_ref)

    for p in range(pairs):
        heads = []
        for h in (2 * p, 2 * p + 1):
            heads.append(acc_ref[h, 0:HEAD_DIM, :] / acc_ref[h, HEAD_DIM:HEAD_DIM + 1, :])
        y_ref[:, p * LANES:(p + 1) * LANES] = jnp.concatenate(heads, axis=0).T.astype(BF16)


def _fox(z, kf, batch, seq, tile):
    nq = seq // tile
    T = batch * seq
    return pl.pallas_call(
        functools.partial(_fox_kernel, tq=tile, tk=tile),
        grid=(batch, nq),
        in_specs=[
            pl.BlockSpec((tile, BRANCH_WIDTH), lambda b, i: (b * nq + i, CB_QA // 4)),
            pl.BlockSpec((seq, HEADS * LANES), lambda b, i: (b, 0)),
            pl.BlockSpec((seq, BRANCH_WIDTH), lambda b, i: (b, CB_VA // 4)),
        ],
        out_specs=pl.BlockSpec((tile, BRANCH_WIDTH), lambda b, i: (b * nq + i, 0)),
        out_shape=jax.ShapeDtypeStruct((T, BRANCH_WIDTH), BF16),
        scratch_shapes=[
            pltpu.VMEM((HEADS, seq // tile, VT_ROWS, tile), BF16),
            pltpu.VMEM((HEADS, LANES, tile), BF16),
            pltpu.VMEM((HEADS, tile, tile), F32),
            pltpu.VMEM((HEADS, tile, tile), F32),
            pltpu.VMEM((HEADS, 1, tile), F32),
            pltpu.VMEM((HEADS, VT_ROWS, tile), F32),
        ],
        compiler_params=pltpu.CompilerParams(dimension_semantics=("parallel", "arbitrary"), vmem_limit_bytes=VMEM_LIMIT),
        name="fox",
    )(z, kf, z)


def _half_rows_operand(qt, first_half):
    row = lax.broadcasted_iota(I32, qt.shape, 0)
    own = (row < HEAD_DIM) if first_half else (row >= HEAD_DIM)
    return jnp.where(own, qt, 0.0).astype(BF16)


def _bit_planes(key_u):
    a = [key_u[SUBLANES * m:SUBLANES * (m + 1), :] for m in range(WORD_BITS)]
    j, mask = WORD_BITS // 2, 0x0000FFFF
    while j:
        k = 0
        while k < WORD_BITS:
            t = (a[k] ^ lax.shift_right_logical(a[k + j], jnp.int32(j))) & _i32(mask)
            a[k] = a[k] ^ t
            a[k + j] = a[k + j] ^ (t << j)
            k = (k + j + 1) & ~j
        j >>= 1
        if j:
            mask = (mask ^ (mask << j)) & 0xFFFFFFFF
    return a


def _i32(u):
    return jnp.int32(u - (1 << 32) if u >= (1 << 31) else u)


def _store_bit_planes(plane_ref, kc, key):
    for p, plane in enumerate(_bit_planes(key ^ INT_MIN)):
        plane_ref[kc, p] = plane


def _word_mask_of_rows(valid):
    words = jnp.zeros((SUBLANES, valid.shape[1]), I32)
    for m in range(WORD_BITS):
        bit = _i32(1 << (WORD_BITS - 1 - m))
        words = words | jnp.where(valid[SUBLANES * m:SUBLANES * (m + 1), :], bit, 0)
    return words


def _topk_threshold(plane_ref, candidates, topk):
    tq = plane_ref.shape[3]

    def bit_body(p, carry):
        masks, want, v_u = carry
        ones = [m & plane_ref[kc, p] for kc, m in enumerate(masks)]
        count = jnp.zeros((SUBLANES, tq), I32)
        for t in ones:
            count = count + lax.population_count(t)
        count = jnp.sum(count, axis=0, keepdims=True)
        take = count >= want
        masks = [jnp.where(take, t, m ^ t) for t, m in zip(ones, masks)]
        want = jnp.where(take, want, want - count)
        v_u = v_u | jnp.where(take, jnp.left_shift(jnp.int32(1), WORD_BITS - 1 - p), 0)
        return masks, want, v_u

    init = (list(candidates), jnp.full((1, tq), topk, I32), jnp.zeros((1, tq), I32))
    _, want, v_u = lax.fori_loop(0, WORD_BITS, bit_body, init)
    return v_u ^ INT_MIN, want.astype(F32)


def _topk_bias(key_ref, plane_ref, bias_ref, tril_ref, candidates, n_chunks, topk):
    tq = key_ref.shape[2]
    v, need = _topk_threshold(plane_ref, candidates, topk)

    def bias_body(kc, carry):
        key = key_ref[kc]
        eq = (key == v) & (key > INT_MIN)
        eqf = jnp.where(eq, 1.0, 0.0)
        prefix = _dot(tril_ref[...], eqf.astype(BF16)) + carry
        sel = (key > v) | (eq & (prefix <= need))
        bias_ref[kc] = jnp.where(sel, 0.0, MASKED)
        return carry + jnp.sum(eqf, axis=0, keepdims=True)

    lax.fori_loop(0, n_chunks, bias_body, jnp.zeros((1, tq), F32))


def _dsa_kernel(qi_ref, kk_ref, vv_ref, qb_ref, zs_ref, tril_ref, y_ref,
                key_ref, plane_ref, bias_ref, vt_ref, qit_ref, qbt_ref, sa_ref, sb_ref, m_ref, acc_ref, *, tq, ck, topk):
    i = pl.program_id(1)
    n_chunks = i + 1
    n_kc_total = kk_ref.shape[0] // ck

    @pl.when(i == 0)
    def _():
        ones_row = jnp.where(lax.broadcasted_iota(I32, (VT_ROWS - HEAD_DIM, ck), 0) == 0, 1.0, 0.0).astype(BF16)
        for c in range(n_kc_total):
            t = vv_ref[c * ck:(c + 1) * ck, :].astype(F32).T.astype(BF16)
            vt_ref[c, 0:HEAD_DIM, :] = t[0:HEAD_DIM, :]
            vt_ref[c, HEAD_DIM:VT_ROWS, :] = ones_row

    for p in range(IDX_HEADS // 2):
        qt = qi_ref[:, p * LANES:(p + 1) * LANES].astype(F32).T
        qit_ref[2 * p] = _half_rows_operand(qt, True)
        qit_ref[2 * p + 1] = _half_rows_operand(qt, False)
    for p in range(HEADS // 2):
        qt = qb_ref[:, p * LANES:(p + 1) * LANES].astype(F32).T
        qbt_ref[2 * p] = _half_rows_operand(qt, True)
        qbt_ref[2 * p + 1] = _half_rows_operand(qt, False)
    w_scale = (IDX_HEADS ** -0.5) * (HEAD_DIM ** -0.5)
    w_rows = zs_ref[...].T[ZS_WI:ZS_WI + IDX_HEADS, :] * w_scale

    key_pos = i * ck + lax.broadcasted_iota(I32, (ck, 1), 0)
    query_pos = i * tq + lax.broadcasted_iota(I32, (1, tq), 1)
    causal = key_pos <= query_pos

    def score_chunk(kc, diagonal):
        k0 = pl.multiple_of(kc * ck, ck)
        ki = kk_ref[pl.ds(k0, ck), LANES:2 * LANES]
        s = jnp.zeros((ck, tq), F32)
        for h in range(IDX_HEADS):
            s = s + jnp.maximum(_dot(ki, qit_ref[h]), 0.0) * w_rows[h:h + 1, :]
        s = jnp.where(s == 0.0, 0.0, s)
        bits = pltpu.bitcast(s, I32)
        key = bits ^ ((bits >> 31) & 0x7FFFFFFF)
        if diagonal:
            key = jnp.where(causal, key, INT_MIN)
        key_ref[kc] = key
        _store_bit_planes(plane_ref, kc, key)

    def score_body(kc, carry):
        score_chunk(kc, False)
        return carry

    lax.fori_loop(0, i, score_body, 0)
    score_chunk(i, True)

    diag_words = _word_mask_of_rows(causal)
    candidates = [jnp.where(kc < i, -1, jnp.where(kc == i, diag_words, 0)) for kc in range(n_kc_total)]
    _topk_bias(key_ref, plane_ref, bias_ref, tril_ref, candidates, n_chunks, topk)

    m_ref[...] = jnp.full(m_ref.shape, M_INIT, F32)
    acc_ref[...] = jnp.zeros(acc_ref.shape, F32)

    def scores(h, kc, diagonal):
        k0 = pl.multiple_of(kc * ck, ck)
        return _dot(kk_ref[pl.ds(k0, ck), 0:LANES], qbt_ref[h]) + bias_ref[kc]

    _flash_chunks(i, scores, lambda h, kc: vt_ref[kc], sa_ref, sb_ref, m_ref, acc_ref)

    for p in range(HEADS // 2):
        heads = []
        for h in (2 * p, 2 * p + 1):
            heads.append(acc_ref[h, 0:HEAD_DIM, :] / acc_ref[h, HEAD_DIM:HEAD_DIM + 1, :])
        y_ref[:, p * LANES:(p + 1) * LANES] = jnp.concatenate(heads, axis=0).T.astype(BF16)


def _dsa(z, zs, batch, seq, tile):
    nq = seq // tile
    T = batch * seq
    topk = min(DSA_TOPK_MAX, seq // 4)
    tril = jnp.asarray(np.tril(np.ones((tile, tile), np.float32)), BF16)
    return pl.pallas_call(
        functools.partial(_dsa_kernel, tq=tile, ck=tile, topk=topk),
        grid=(batch, nq),
        in_specs=[
            pl.BlockSpec((tile, 2 * LANES), lambda b, i: (b * nq + i, CB_QI // 2)),
            pl.BlockSpec((seq, 2 * LANES), lambda b, i: (b, CB_KB // 2)),
            pl.BlockSpec((seq, LANES), lambda b, i: (b, CB_VB)),
            pl.BlockSpec((tile, BRANCH_WIDTH), lambda b, i: (b * nq + i, CB_QB // 4)),
            pl.BlockSpec((tile, LANES), lambda b, i: (b * nq + i, 0)),
            pl.BlockSpec((tile, tile), lambda b, i: (0, 0)),
        ],
        out_specs=pl.BlockSpec((tile, BRANCH_WIDTH), lambda b, i: (b * nq + i, 0)),
        out_shape=jax.ShapeDtypeStruct((T, BRANCH_WIDTH), BF16),
        scratch_shapes=[
            pltpu.VMEM((seq // tile, tile, tile), I32),
            pltpu.VMEM((seq // tile, WORD_BITS, SUBLANES, tile), I32),
            pltpu.VMEM((seq // tile, tile, tile), F32),
            pltpu.VMEM((seq // tile, VT_ROWS, tile), BF16),
            pltpu.VMEM((IDX_HEADS, LANES, tile), BF16),
            pltpu.VMEM((HEADS, LANES, tile), BF16),
            pltpu.VMEM((HEADS, tile, tile), F32),
            pltpu.VMEM((HEADS, tile, tile), F32),
            pltpu.VMEM((HEADS, 1, tile), F32),
            pltpu.VMEM((HEADS, VT_ROWS, tile), F32),
        ],
        compiler_params=pltpu.CompilerParams(dimension_semantics=("parallel", "arbitrary"), vmem_limit_bytes=VMEM_LIMIT),
        name="dsa",
    )(z, z, z, z, zs, tril)


def _ret_kernel(q_ref, k_ref, v_ref, g_ref, dm_ref, xi_ref, zeta_ref, decay_ref, gain_ref, y_ref):
    C = RET_CHUNK
    n_chunks = q_ref.shape[0] // C
    lo_half = _lane_lt_half_tile((C, LANES))
    decay = decay_ref[0]
    block_diag = jnp.where(decay > 0.0, 1.0, 0.0)

    def head_stat(x):
        s_lo = jnp.sum(jnp.where(lo_half, x, 0.0), axis=-1, keepdims=True)
        s_hi = jnp.sum(jnp.where(lo_half, 0.0, x), axis=-1, keepdims=True)
        return jnp.where(lo_half, s_lo, s_hi) * (1.0 / HEAD_DIM)

    def group(gi, state):
        rows = [pl.ds(pl.multiple_of((gi * RET_GROUP + c) * C, C), C) for c in range(RET_GROUP)]
        q = [q_ref[r, :] for r in rows]
        k = [k_ref[r, :] for r in rows]
        v = [v_ref[r, :] for r in rows]
        qk, upd = [], []
        for c in range(RET_GROUP):
            zero = jnp.zeros_like(q[c])
            heads = (jnp.where(lo_half, q[c], zero), jnp.where(lo_half, zero, q[c]))
            qk.append([_dot_nt(heads[e], k[c]) for e in range(2)])
            kz = (k[c].astype(F32) * zeta_ref[0]).T.astype(BF16)
            upd.append(_dot(kz, v[c]) * block_diag)
        cross = []
        for c in range(RET_GROUP):
            cross.append(_dot(q[c], state.astype(BF16)) * xi_ref[0])
            state = state * decay + upd[c]
        for c in range(RET_GROUP):
            inner = [_dot((qk[c][e] * dm_ref[e]).astype(BF16), v[c]) for e in range(2)]
            r = jnp.where(lo_half, inner[0], inner[1]) + cross[c]
            mu = head_stat(r)
            d = r - mu
            var = head_stat(d * d)
            y = d * lax.rsqrt(var + EPS) * gain_ref[...]
            gate = g_ref[rows[c], :].astype(F32)
            y_ref[rows[c], :] = (y * (gate * jax.nn.sigmoid(gate))).astype(BF16)
        return state

    lax.fori_loop(0, n_chunks // RET_GROUP, group, jnp.zeros((LANES, LANES), F32))


def _retention_tables():
    C = RET_CHUNK
    log_g = np.log(1.0 - 2.0 ** (-5.0 - np.arange(HEADS, dtype=np.float64)))
    n = np.arange(C, dtype=np.float64)
    diff = n[:, None] - n[None, :]
    dm = np.where(diff[None] >= 0, np.exp(diff[None] * log_g[:, None, None]), 0.0)
    xi = np.exp((n + 1.0)[None, :] * log_g[:, None])
    zeta = np.exp((C - 1.0 - n)[None, :] * log_g[:, None])
    g_c = np.exp(C * log_g)
    pair = lambda t: np.repeat(t.reshape(HEADS // 2, 2, C), HEAD_DIM, axis=1).transpose(0, 2, 1)
    decay = np.zeros((HEADS // 2, LANES, LANES))
    for p in range(HEADS // 2):
        decay[p, :HEAD_DIM, :HEAD_DIM] = g_c[2 * p]
        decay[p, HEAD_DIM:, HEAD_DIM:] = g_c[2 * p + 1]
    f = lambda t: jnp.asarray(t.astype(np.float32))
    return f(dm), f(pair(xi)), f(pair(zeta)), f(decay)


def _retention(z, gain_row, batch, seq):
    T = batch * seq
    C = RET_CHUNK
    dm, xi, zeta, decay = _retention_tables()
    pairs = HEADS // 2
    return pl.pallas_call(
        _ret_kernel,
        grid=(batch, pairs),
        in_specs=[
            pl.BlockSpec((seq, LANES), lambda b, p: (b, CB_QC + p)),
            pl.BlockSpec((seq, LANES), lambda b, p: (b, CB_KC + p)),
            pl.BlockSpec((seq, LANES), lambda b, p: (b, CB_VC + p)),
            pl.BlockSpec((seq, LANES), lambda b, p: (b, CB_GC + p)),
            pl.BlockSpec((2, C, C), lambda b, p: (p, 0, 0)),
            pl.BlockSpec((1, C, LANES), lambda b, p: (p, 0, 0)),
            pl.BlockSpec((1, C, LANES), lambda b, p: (p, 0, 0)),
            pl.BlockSpec((1, LANES, LANES), lambda b, p: (p, 0, 0)),
            pl.BlockSpec((1, LANES), lambda b, p: (0, p)),
        ],
        out_specs=pl.BlockSpec((seq, LANES), lambda b, p: (b, p)),
        out_shape=jax.ShapeDtypeStruct((T, BRANCH_WIDTH), BF16),
        compiler_params=pltpu.CompilerParams(dimension_semantics=("parallel", "parallel")),
        name="retention",
    )(z, z, z, z, dm, xi, zeta, decay, gain_row)


def _swa_kernel(q_ref, kc_ref, kp_ref, vc_ref, vp_ref, sink_ref, y_ref, *, blocks):
    n = pl.program_id(1)
    W = SWA_WINDOW
    G = HEADS // SWA_KV_HEADS
    lo_half = _lane_lt_half_tile((W, LANES))
    qi = lax.broadcasted_iota(I32, (W, 2 * W), 0)
    kj = lax.broadcasted_iota(I32, (W, 2 * W), 1)
    in_band = (kj > qi) & (kj <= qi + W)
    chains = [(j, g) for j in range(blocks) for g in range(SWA_KV_HEADS)]

    def band_of(ref_prev, ref_cur, j, cols):
        if j == 0:
            return jnp.concatenate([ref_prev[:, cols], ref_cur[0:W, cols]], axis=0)
        return ref_cur[(j - 1) * W:(j + 1) * W, cols]

    sinks = {}
    for g in range(SWA_KV_HEADS):
        sinks[g] = jnp.concatenate(
            [jnp.broadcast_to(sink_ref[:, G * g + r:G * g + r + 1], (W, 1)) for r in range(G)], axis=0)

    scores = {}
    for j, g in chains:
        cols = slice(g * LANES, (g + 1) * LANES)
        q_rows = []
        for r in range(G):
            h = G * g + r
            t = q_ref[j * W:(j + 1) * W, (h // 2) * LANES:(h // 2 + 1) * LANES]
            keep = lo_half if h % 2 == 0 else jnp.logical_not(lo_half)
            q_rows.append(jnp.where(keep, t, jnp.zeros_like(t)))
        s = _dot_nt(jnp.concatenate(q_rows, axis=0), band_of(kp_ref, kc_ref, j, cols)).reshape(G, W, 2 * W)
        band = in_band & ((kj >= W) | (n > 0)) if j == 0 else in_band
        scores[j, g] = jnp.where(band[None], s, MASKED).reshape(G * W, 2 * W)

    probs, sink_terms = {}, {}
    for j, g in chains:
        s = scores[j, g]
        m = jnp.maximum(jnp.max(s, axis=-1, keepdims=True), sinks[g])
        probs[j, g] = jnp.exp(s - m).astype(BF16)
        sink_terms[j, g] = jnp.exp(sinks[g] - m)

    ones = jnp.ones((2 * W, LANES), BF16)
    for j, g in chains:
        cols = slice(g * LANES, (g + 1) * LANES)
        denom = _dot(probs[j, g], ones) + sink_terms[j, g]
        o = _dot(probs[j, g], band_of(vp_ref, vc_ref, j, cols)) / denom
        for r2 in range(G // 2):
            even = o[(2 * r2) * W:(2 * r2 + 1) * W, :]
            odd = o[(2 * r2 + 1) * W:(2 * r2 + 2) * W, :]
            pcol = (G * g) // 2 + r2
            y_ref[j * W:(j + 1) * W, pcol * LANES:(pcol + 1) * LANES] = jnp.where(lo_half, even, odd).astype(BF16)


def _swa(z, sink_row, batch, seq):
    T = batch * seq
    W = SWA_WINDOW
    blocks = min(SWA_BLOCKS, seq // W)
    rows = blocks * W
    ns = seq // rows
    cur = lambda cb: (lambda b, n: (b * ns + n, cb))
    prev = lambda cb: (lambda b, n: (jnp.maximum((b * ns + n) * blocks - 1, b * ns * blocks), cb))
    return pl.pallas_call(
        functools.partial(_swa_kernel, blocks=blocks),
        grid=(batch, ns),
        in_specs=[
            pl.BlockSpec((rows, BRANCH_WIDTH), cur(CB_QD // 4)),
            pl.BlockSpec((rows, 2 * LANES), cur(CB_KD // 2)),
            pl.BlockSpec((W, 2 * LANES), prev(CB_KD // 2)),
            pl.BlockSpec((rows, 2 * LANES), cur(CB_VD // 2)),
            pl.BlockSpec((W, 2 * LANES), prev(CB_VD // 2)),
            pl.BlockSpec((1, LANES), lambda b, n: (0, 0)),
        ],
        out_specs=pl.BlockSpec((rows, BRANCH_WIDTH), lambda b, n: (b * ns + n, 0)),
        out_shape=jax.ShapeDtypeStruct((T, BRANCH_WIDTH), BF16),
        compiler_params=pltpu.CompilerParams(dimension_semantics=("parallel", "arbitrary"), vmem_limit_bytes=VMEM_LIMIT),
        name="swa",
    )(z, z, z, z, z, sink_row)


def _merge_kernel(x_ref, g_ref, ya_ref, yb_ref, yc_ref, yd_ref, wg_ref, wb_ref, wo_ref, o_ref):
    x = x_ref[...]
    D = x.shape[1]
    hb = _rmsnorm_f32(x, g_ref[...]).astype(BF16)
    merged = jnp.zeros(x.shape, F32)
    for n, y_ref in enumerate((ya_ref, yb_ref, yc_ref, yd_ref)):
        gate = jax.nn.sigmoid(_dot(hb, wg_ref[:, n * D:(n + 1) * D]))
        merged = merged + gate * _dot(y_ref[...], wb_ref[n])
    o_ref[...] = x + _dot(merged.astype(BF16), wo_ref[...])


def _merge(x2, g, ys, w_gates, w_branch, w_out):
    T, D = x2.shape
    tm = min(512, T)
    const = lambda shape: pl.BlockSpec(shape, lambda i: (0,) * len(shape), pipeline_mode=pl.Buffered(1))
    y_spec = pl.BlockSpec((tm, BRANCH_WIDTH), lambda i: (i, 0))
    return pl.pallas_call(
        _merge_kernel,
        grid=(T // tm,),
        in_specs=[
            pl.BlockSpec((tm, D), lambda i: (i, 0)),
            const((1, D)),
            y_spec, y_spec, y_spec, y_spec,
            const((D, N_BRANCH * D)),
            const((N_BRANCH, BRANCH_WIDTH, D)),
            const((D, D)),
        ],
        out_specs=pl.BlockSpec((tm, D), lambda i: (i, 0)),
        out_shape=jax.ShapeDtypeStruct((T, D), F32),
        compiler_params=pltpu.CompilerParams(dimension_semantics=("parallel",), vmem_limit_bytes=VMEM_LIMIT),
        name="merge",
    )(x2, g, *ys, w_gates, w_branch, w_out)


def _ffn_kernel(x_ref, g_ref, wup_ref, cw_ref, cb_ref, wdn_ref, fg_ref, o_ref, carry_ref, u_ref, act_ref,
                *, tc, final_norm):
    n = pl.program_id(1)
    tm = x_ref.shape[0]
    d_ff = wdn_ref.shape[0]
    PAD = 8
    n_slots = u_ref.shape[0]

    @pl.when(n == 0)
    def _():
        carry_ref[...] = jnp.zeros_like(carry_ref)

    x = x_ref[...]
    hb = _rmsnorm_f32(x, g_ref[...]).astype(BF16)

    def conv(col0, slot):
        cols = slice(col0, col0 + tc)
        u = _dot(hb, wup_ref[:, cols])
        u_ref[slot, 0:PAD, :] = carry_ref[:, cols]
        u_ref[slot, PAD:PAD + tm, :] = u
        carry_ref[:, cols] = u[tm - PAD:tm, :]
        w = cw_ref[:, cols]
        return (w[2:3, :] * u + w[1:2, :] * u_ref[slot, PAD - 1:PAD - 1 + tm, :]
                + w[0:1, :] * u_ref[slot, PAD - 2:PAD - 2 + tm, :] + cb_ref[:, cols])

    for c in range(d_ff // tc):
        a = conv(c * tc, (2 * c) % n_slots)
        b = conv(d_ff + c * tc, (2 * c + 1) % n_slots)
        act_ref[:, c * tc:(c + 1) * tc] = (a * jax.nn.sigmoid(a) * b).astype(BF16)
    y = x + _dot(act_ref[...], wdn_ref[...])
    if final_norm:
        y = _rmsnorm_f32(y, fg_ref[...])
    o_ref[...] = y


def _ffn(x2, g, w_up, conv_w, conv_b, w_down, final_g, batch, seq, final_norm):
    T, D = x2.shape
    d_ff = w_down.shape[0]
    tm = min(512, seq)
    tc = 256
    nb = seq // tm
    const = lambda shape: pl.BlockSpec(shape, lambda b, n: (0,) * len(shape), pipeline_mode=pl.Buffered(1))
    return pl.pallas_call(
        functools.partial(_ffn_kernel, tc=tc, final_norm=final_norm),
        grid=(batch, nb),
        in_specs=[
            pl.BlockSpec((tm, D), lambda b, n: (b * nb + n, 0)),
            const((1, D)),
            const((D, 2 * d_ff)),
            const((CONV_WIDTH, 2 * d_ff)),
            const((1, 2 * d_ff)),
            const((d_ff, D)),
            const((1, D)),
        ],
        out_specs=pl.BlockSpec((tm, D), lambda b, n: (b * nb + n, 0)),
        out_shape=jax.ShapeDtypeStruct((T, D), F32),
        scratch_shapes=[pltpu.VMEM((8, 2 * d_ff), F32), pltpu.VMEM((FFN_SLOTS, tm + 8, tc), F32),
                        pltpu.VMEM((tm, d_ff), BF16)],
        compiler_params=pltpu.CompilerParams(dimension_semantics=("parallel", "arbitrary"), vmem_limit_bytes=VMEM_LIMIT),
        name="ffn",
    )(x2, g, w_up, conv_w, conv_b, w_down, final_g)


def _rope_tables(seq):
    pos = jnp.arange(seq, dtype=F32)
    inv = 1.0 / (ROPE_THETA ** (jnp.arange(HALF, dtype=F32) * 2.0 / HEAD_DIM))
    ang = pos[:, None] * inv[None, :]
    cos, sin = jnp.cos(ang), jnp.sin(ang)
    cos_t = jnp.tile(cos, (1, LANES // HALF))
    sin_t = jnp.tile(jnp.concatenate([-sin, sin], axis=1), (1, LANES // HEAD_DIM))
    return cos_t, sin_t


def kernel(x, attn_norm, w_in, forget_bias, ret_norm, attn_sinks, w_branch, w_out, ffn_norm, w_up, conv_w, conv_b, w_down, final_norm):
    B, S, D = x.shape
    depth = w_in.shape[0]
    T = B * S
    cos_t, sin_t = _rope_tables(S)
    tile_fox = min(256, S)
    tile_dsa = min(256, S)
    pad_row = lambda v: jnp.pad(v.astype(F32), (0, LANES - v.shape[0]))[None, :]

    x2 = x.reshape(T, D)
    for l in range(depth):
        w_main, w_small, w_gates = _repack_w_in(w_in, l)

        z, zs = _inproj(x2, attn_norm[l][None, :], w_main, w_small, cos_t, sin_t, S)
        kf = _fcum(zs, pad_row(forget_bias[l]), z, S)

        y_a = _fox(z, kf, B, S, tile_fox)
        y_b = _dsa(z, zs, B, S, tile_dsa)
        y_c = _retention(z, ret_norm[l][None, :].astype(F32), B, S)
        y_d = _swa(z, pad_row(attn_sinks[l]), B, S)

        x2 = _merge(x2, attn_norm[l][None, :], (y_a, y_b, y_c, y_d), w_gates,
                    w_branch[l].astype(BF16), w_out[l].astype(BF16))
        x2 = _ffn(x2, ffn_norm[l][None, :], w_up[l].astype(BF16), conv_w[l], conv_b[l][None, :],
                  w_down[l].astype(BF16), final_norm[None, :], B, S, final_norm=(l == depth - 1))
    return x2.reshape(B, S, D)
```
